```python
import jax, jax.numpy as jnp
from jax import lax
import numpy as np

D_MODEL = 2048
BATCH = 4
SEQ = 4096
DEPTH = 2

HEAD_DIM = 128
EPS = 1e-6
ROPE_THETA = 10000.0
N_EVEN = (DEPTH + 1) // 2
N_ODD = DEPTH // 2

POOL_WINDOWS = (2, 4, 8, 16)
POOL_WIDTH = D_MODEL // 2
POOL_GROUP = POOL_WIDTH // len(POOL_WINDOWS)

DIL_PATTERNS = ((128, 1), (512, 4), (2048, 16))
N_DIL_GROUPS = len(DIL_PATTERNS)
DIL_HEADS = (D_MODEL - POOL_WIDTH) // HEAD_DIM
DIL_QBLOCK = 128
AB_IN = POOL_WIDTH + 3 * N_DIL_GROUPS * DIL_HEADS * HEAD_DIM

SB_HEADS = D_MODEL // HEAD_DIM // 2
SB_QBLOCK = 128
MOBA_HEADS = D_MODEL // HEAD_DIM - SB_HEADS
MOBA_BLOCK = 256
MOBA_TOPK = 3
MOBA_QCHUNK = 16
CD_IN = 3 * (SB_HEADS + MOBA_HEADS) * HEAD_DIM

FFN_HIDDEN = -(-8 * D_MODEL // (3 * 256)) * 256

kernel_name = "hybrid_pool_dilated_stickbreak_moba_block"


def rms_norm(x, g):
    xf = x.astype(jnp.float32)
    y = xf * lax.rsqrt(jnp.mean(xf * xf, axis=-1, keepdims=True) + EPS)
    return (y * g.astype(jnp.float32)).astype(x.dtype)


def rope_tables(seq):
    inv = 1.0 / (ROPE_THETA ** (jnp.arange(0, HEAD_DIM, 2, dtype=jnp.float32) / HEAD_DIM))
    ang = jnp.arange(seq, dtype=jnp.float32)[:, None] * inv[None, :]
    return jnp.cos(ang), jnp.sin(ang)


def apply_rope(x, cos, sin):
    x1, x2 = jnp.split(x.astype(jnp.float32), 2, axis=-1)
    return jnp.concatenate([x1 * cos - x2 * sin, x2 * cos + x1 * sin], axis=-1).astype(x.dtype)


def pool_mixer(u, pool_w, pool_scale):
    Bn, S, _ = u.shape
    uf = u.astype(jnp.float32).reshape(Bn, S, len(POOL_WINDOWS), POOL_GROUP)
    c = jnp.pad(jnp.cumsum(uf, axis=1), ((0, 0), (1, 0), (0, 0), (0, 0)))
    t = jnp.arange(S)
    pooled = []
    for g, w in enumerate(POOL_WINDOWS):
        lo = jnp.maximum(t + 1 - w, 0)
        cg = c[:, :, g]
        cnt = (t + 1 - lo).astype(jnp.float32)[None, :, None]
        pooled.append((cg[:, 1:] - cg[:, lo]) / cnt - uf[:, :, g])
    pooled = jnp.stack(pooled, axis=2)
    y = jnp.einsum('bsgc,gcd->bsgd', pooled.astype(u.dtype), pool_w)
    return y.reshape(Bn, S, POOL_WIDTH) * pool_scale


def dilated_branch(q, k, v, dil, band):
    Bn, H, S, dh = q.shape
    L = S // dil
    nblk = -(-L // DIL_QBLOCK)
    Lp = nblk * DIL_QBLOCK
    QB = DIL_QBLOCK

    def to_blocks(x):
        xs = x.reshape(Bn, H, L, dil, dh).transpose(0, 1, 3, 2, 4)
        xs = jnp.pad(xs, ((0, 0), (0, 0), (0, 0), (0, Lp - L), (0, 0)))
        return xs.reshape(Bn, H, dil, nblk, QB, dh)

    qb, kb, vb = to_blocks(q), to_blocks(k), to_blocks(v)
    prev = lambda x: jnp.pad(x, ((0, 0),) * 3 + ((1, 0), (0, 0), (0, 0)))[:, :, :, :-1]
    kk = jnp.concatenate([prev(kb), kb], axis=4)
    vv = jnp.concatenate([prev(vb), vb], axis=4)
    s = jnp.einsum('bhrnqe,bhrnke->bhrnqk', qb, kk).astype(jnp.float32) * (HEAD_DIM ** -0.5)
    n = jnp.arange(nblk)[:, None, None]
    qg = n * QB + jnp.arange(QB)[None, :, None]
    kg = n * QB + jnp.arange(2 * QB)[None, None, :] - QB
    dist = qg - kg
    mask = (dist >= 0) & (dist <= band) & (kg >= 0)
    s = jnp.where(mask, s, -jnp.inf)
    m = jnp.max(s, axis=-1, keepdims=True)
    p = jnp.exp(s - m)
    den = jnp.sum(p, axis=-1, keepdims=True)
    o = jnp.einsum('bhrnqk,bhrnke->bhrnqe', (p / den).astype(v.dtype), vv)
    lse = (m + jnp.log(den))[..., 0]
    o = o.reshape(Bn, H, dil, Lp, dh)[:, :, :, :L].transpose(0, 1, 3, 2, 4).reshape(Bn, H, S, dh)
    lse = lse.reshape(Bn, H, dil, Lp)[:, :, :, :L].transpose(0, 1, 3, 2).reshape(Bn, H, S)
    return o, lse


def stick_breaking(q, k, v):
    Bn, H, S, dh = q.shape
    nblk = S // SB_QBLOCK
    qb = q.reshape(Bn, H, nblk, SB_QBLOCK, dh).transpose(2, 0, 1, 3, 4)
    kpos = jnp.arange(S)

    def block(args):
        qi, i = args
        z = jnp.einsum('bhqe,bhke->bhqk', qi, k).astype(jnp.float32) * (HEAD_DIM ** -0.5)
        qpos = i * SB_QBLOCK + jnp.arange(SB_QBLOCK)
        mask = kpos[None, :] < qpos[:, None]
        log1m = jnp.where(mask, -jax.nn.softplus(z), 0.0)
        suffix = lax.cumsum(log1m, axis=3, reverse=True) - log1m
        a = jnp.where(mask, jnp.exp(jax.nn.log_sigmoid(z) + suffix), 0.0)
        return jnp.einsum('bhqk,bhke->bhqe', a.astype(v.dtype), v)

    out = lax.map(block, (qb, jnp.arange(nblk)))
    return out.transpose(1, 2, 0, 3, 4).reshape(Bn, H, S, dh)


def moba_attention(q, k, v):
    Bn, H, S, dh = q.shape
    nb = -(-S // MOBA_BLOCK)
    Sp = nb * MOBA_BLOCK
    padk = ((0, 0), (0, 0), (0, Sp - S), (0, 0))
    kblk = jnp.pad(k, padk).reshape(Bn, H, nb, MOBA_BLOCK, dh)
    vblk = jnp.pad(v, padk).reshape(Bn, H, nb, MOBA_BLOCK, dh)
    kmean = jnp.mean(kblk.astype(jnp.float32), axis=3)
    topk = min(MOBA_TOPK, nb)
    nq = S // MOBA_QCHUNK
    qc = q.reshape(Bn, H, nq, MOBA_QCHUNK, dh).transpose(2, 0, 1, 3, 4)
    bi = jnp.arange(Bn)[:, None, None, None]
    hi = jnp.arange(H)[None, :, None, None]
    scale = HEAD_DIM ** -0.5

    def chunk(args):
        qi, c = args
        qpos = c * MOBA_QCHUNK + jnp.arange(MOBA_QCHUNK)
        ob = (c * MOBA_QCHUNK) // MOBA_BLOCK
        gate = jnp.einsum('bhqe,bhne->bhqn', qi.astype(jnp.float32), kmean)
        past = jnp.arange(nb)[None, :] < ob
        gate = jnp.where(past, gate, -jnp.inf)
        gval, gidx = lax.top_k(gate, topk)
        valid = jnp.isfinite(gval)
        ksel = kblk[bi, hi, gidx]
        vsel = vblk[bi, hi, gidx]
        s_sel = jnp.einsum('bhqe,bhqnke->bhqnk', qi, ksel).astype(jnp.float32) * scale
        s_sel = jnp.where(valid[..., None], s_sel, -jnp.inf).reshape(Bn, H, MOBA_QCHUNK, topk * MOBA_BLOCK)
        kown = lax.dynamic_index_in_dim(kblk, ob, axis=2, keepdims=False)
        vown = lax.dynamic_index_in_dim(vblk, ob, axis=2, keepdims=False)
        s_own = jnp.einsum('bhqe,bhke->bhqk', qi, kown).astype(jnp.float32) * scale
        kpos = ob * MOBA_BLOCK + jnp.arange(MOBA_BLOCK)
        s_own = jnp.where(kpos[None, :] <= qpos[:, None], s_own, -jnp.inf)
        p = jax.nn.softmax(jnp.concatenate([s_sel, s_own], axis=-1), axis=-1).astype(v.dtype)
        p_sel = p[..., :topk * MOBA_BLOCK].reshape(Bn, H, MOBA_QCHUNK, topk, MOBA_BLOCK)
        p_own = p[..., topk * MOBA_BLOCK:]
        return (jnp.einsum('bhqnk,bhqnke->bhqe', p_sel, vsel)
                + jnp.einsum('bhqk,bhke->bhqe', p_own, vown))

    out = lax.map(chunk, (qc, jnp.arange(nq)))
    return out.transpose(1, 2, 0, 3, 4).reshape(Bn, H, S, dh)


def mix_ab(h, w_in, pool_w, pool_scale, w_out, cos, sin):
    Bn, S, _ = h.shape
    proj = h @ w_in
    a_out = pool_mixer(proj[..., :POOL_WIDTH], pool_w, pool_scale)
    qkv = proj[..., POOL_WIDTH:].reshape(Bn, S, 3, N_DIL_GROUPS, DIL_HEADS, HEAD_DIM)
    qkv = qkv.transpose(2, 3, 0, 4, 1, 5)
    outs, lses = [], []
    for g, (window, dil) in enumerate(DIL_PATTERNS):
        q = apply_rope(qkv[0, g], cos, sin)
        k = apply_rope(qkv[1, g], cos, sin)
        o, lse = dilated_branch(q, k, qkv[2, g], dil, window // dil)
        outs.append(o)
        lses.append(lse)
    wts = jax.nn.softmax(jnp.stack(lses, axis=0), axis=0)
    o = jnp.sum(wts[..., None] * jnp.stack(outs, axis=0).astype(jnp.float32), axis=0)
    b_out = o.transpose(0, 2, 1, 3).reshape(Bn, S, DIL_HEADS * HEAD_DIM).astype(h.dtype)
    return jnp.concatenate([a_out, b_out], axis=-1) @ w_out


def mix_cd(h, w_in, w_out, cos, sin):
    Bn, S, _ = h.shape
    proj = h @ w_in
    c_w = 3 * SB_HEADS * HEAD_DIM
    sb = proj[..., :c_w].reshape(Bn, S, 3, SB_HEADS, HEAD_DIM).transpose(2, 0, 3, 1, 4)
    mb = proj[..., c_w:].reshape(Bn, S, 3, MOBA_HEADS, HEAD_DIM).transpose(2, 0, 3, 1, 4)
    c_out = stick_breaking(sb[0], sb[1], sb[2])
    d_out = moba_attention(apply_rope(mb[0], cos, sin), apply_rope(mb[1], cos, sin), mb[2])
    cat = jnp.concatenate([c_out, d_out], axis=1)
    return cat.transpose(0, 2, 1, 3).reshape(Bn, S, D_MODEL) @ w_out


def swiglu(h, w_gate, w_up, w_down):
    return (jax.nn.silu(h @ w_gate) * (h @ w_up)) @ w_down


def setup_inputs(seed: int = 0) -> dict:
    key = jax.random.key(seed)
    ks = jax.random.split(key, 12)
    nrm = lambda k, shape, fan: jax.random.normal(k, shape, jnp.float32) * (fan ** -0.5)
    return {
        "x": jax.random.normal(ks[0], (BATCH, SEQ, D_MODEL), jnp.float32),
        "norm_gains": 1.0 + 0.05 * jax.random.normal(ks[1], (DEPTH, 4, D_MODEL), jnp.float32),
        "w_in_ab": nrm(ks[2], (N_EVEN, D_MODEL, AB_IN), D_MODEL),
        "pool_w": nrm(ks[3], (N_EVEN, len(POOL_WINDOWS), POOL_GROUP, POOL_GROUP), POOL_GROUP),
        "pool_scale": 1.0 + 0.1 * jax.random.normal(ks[4], (N_EVEN, POOL_WIDTH), jnp.float32),
        "w_out_ab": nrm(ks[5], (N_EVEN, D_MODEL, D_MODEL), D_MODEL),
        "w_in_cd": nrm(ks[6], (N_ODD, D_MODEL, CD_IN), D_MODEL),
        "w_out_cd": nrm(ks[7], (N_ODD, D_MODEL, D_MODEL), D_MODEL),
        "ffn_gate": nrm(ks[8], (DEPTH, D_MODEL, FFN_HIDDEN), D_MODEL),
        "ffn_up": nrm(ks[9], (DEPTH, D_MODEL, FFN_HIDDEN), D_MODEL),
        "ffn_down": nrm(ks[10], (DEPTH, FFN_HIDDEN, D_MODEL), FFN_HIDDEN),
    }


def reference(x, norm_gains, w_in_ab, pool_w, pool_scale, w_out_ab, w_in_cd, w_out_cd,
              ffn_gate, ffn_up, ffn_down):
    cos, sin = rope_tables(x.shape[1])
    for layer in range(DEPTH):
        g = norm_gains[layer]
        hn = rms_norm(x, g[0])
        i = layer // 2
        if layer % 2 == 0:
            y = mix_ab(hn, w_in_ab[i], pool_w[i], pool_scale[i], w_out_ab[i], cos, sin)
        else:
            y = mix_cd(hn, w_in_cd[i], w_out_cd[i], cos, sin)
        x = x + rms_norm(y, g[1])
        f = swiglu(rms_norm(x, g[2]), ffn_gate[layer], ffn_up[layer], ffn_down[layer])
        x = x + rms_norm(f, g[3])
    return x
```

```python
import functools

import jax
import jax.numpy as jnp
from jax import lax
from jax.experimental import pallas as pl
from jax.experimental.pallas import tpu as pltpu

F32 = jnp.float32
BF16 = jnp.bfloat16

HEAD_DIM = 128
EPS = 1e-6
ROPE_THETA = 10000.0
POOL_WINDOWS = (2, 4, 8, 16)
POOL_GROUP = 256
POOL_HALO = 16
DIL_PATTERNS = ((128, 1), (512, 4), (2048, 16))
DIL_QBLOCK = 128
ATT_HEADS = 8
MOBA_BLOCK = 256
MOBA_TOPK = 3
SB_BLOCK = 256
MASK_VALUE = -1e30
MOBA_UNSELECTED = -30000.0

V7X_VMEM_BYTES = 64 * 1024 * 1024
VMEM_LIMIT = V7X_VMEM_BYTES - 8 * 1024 * 1024


def _params(semantics):
    return pltpu.CompilerParams(dimension_semantics=semantics, vmem_limit_bytes=VMEM_LIMIT)


def _dot(a, b):
    return jnp.dot(a, b, preferred_element_type=F32)


def _dot_nt(a, b):
    return lax.dot_general(a, b, (((1,), (1,)), ((), ())), preferred_element_type=F32)


def _rms_norm(x, gain):
    return x * lax.rsqrt(jnp.mean(x * x, axis=-1, keepdims=True) + EPS) * gain


def _inproj_kernel(x_ref, g_ref, w_ref, cs_ref, cos_ref, sin_ref, o_ref, hn_ref, *, rope_lo, rope_hi):
    j = pl.program_id(1)

    @pl.when(j == 0)
    def _():
        hn_ref[...] = _rms_norm(x_ref[...], g_ref[...]).astype(BF16)

    acc = _dot(hn_ref[...], w_ref[...]) * cs_ref[...]
    is_rope = jnp.logical_and(j >= rope_lo, j < rope_hi)

    @pl.when(is_rope)
    def _():
        cos = cos_ref[...]
        sin = sin_ref[...]
        for c in range(acc.shape[1] // HEAD_DIM):
            blk = acc[:, c * HEAD_DIM:(c + 1) * HEAD_DIM]
            rot = pltpu.roll(blk, HEAD_DIM // 2, 1)
            o_ref[:, c * HEAD_DIM:(c + 1) * HEAD_DIM] = (blk * cos + rot * sin).astype(o_ref.dtype)

    @pl.when(jnp.logical_not(is_rope))
    def _():
        o_ref[...] = acc.astype(o_ref.dtype)


def _inproj(x, gain, w, colscale, cos, sin, *, seq, rope_cols, out_dtype, tm=512, tn=1024):
    t, d = x.shape
    n = w.shape[1]
    assert t % tm == 0 and n % tn == 0 and seq % tm == 0
    assert rope_cols[0] % tn == 0 and rope_cols[1] % tn == 0
    pos_blocks = seq // tm
    kern = functools.partial(_inproj_kernel, rope_lo=rope_cols[0] // tn, rope_hi=rope_cols[1] // tn)
    return pl.pallas_call(
        kern,
        grid=(t // tm, n // tn),
        in_specs=[
            pl.BlockSpec((tm, d), lambda i, j: (i, 0)),
            pl.BlockSpec((1, d), lambda i, j: (0, 0)),
            pl.BlockSpec((d, tn), lambda i, j: (0, j)),
            pl.BlockSpec((1, tn), lambda i, j: (0, j)),
            pl.BlockSpec((tm, HEAD_DIM), lambda i, j: (i % pos_blocks, 0)),
            pl.BlockSpec((tm, HEAD_DIM), lambda i, j: (i % pos_blocks, 0)),
        ],
        out_specs=pl.BlockSpec((tm, tn), lambda i, j: (i, j)),
        out_shape=jax.ShapeDtypeStruct((t, n), out_dtype),
        scratch_shapes=[pltpu.VMEM((tm, d), BF16)],
        compiler_params=_params(("parallel", "arbitrary")),
        name="inproj",
    )(x, gain, w, colscale, cos, sin)


def _pool_kernel(u_ref, halo_ref, w_ref, sc_ref, o_ref, ext_ref, *, tm, seq):
    pos0 = (pl.program_id(0) * tm) % seq
    ext_ref[pl.ds(POOL_HALO, tm), :] = u_ref[...]
    ext_ref[pl.ds(0, POOL_HALO), :] = jnp.where(pos0 == 0, 0.0, halo_ref[...])
    pos = pos0 + lax.broadcasted_iota(jnp.int32, (tm, 1), 0)
    for g, w in enumerate(POOL_WINDOWS):
        cols = pl.ds(g * POOL_GROUP, POOL_GROUP)
        cur = ext_ref[pl.ds(POOL_HALO, tm), cols]
        tot = cur
        for back in range(1, w):
            tot = tot + ext_ref[pl.ds(POOL_HALO - back, tm), cols]
        cnt = jnp.minimum(pos + 1, w).astype(F32)
        pooled = tot / cnt - cur
        y = _dot(pooled.astype(BF16), w_ref[g])
        o_ref[:, g * POOL_GROUP:(g + 1) * POOL_GROUP] = (
            y * sc_ref[:, g * POOL_GROUP:(g + 1) * POOL_GROUP]).astype(o_ref.dtype)


def _pool(proj, w, scale, *, seq, tm=512):
    t = proj.shape[0]
    width = len(POOL_WINDOWS) * POOL_GROUP
    assert seq % tm == 0 and tm % POOL_HALO == 0
    halo_blocks = tm // POOL_HALO
    kern = functools.partial(_pool_kernel, tm=tm, seq=seq)
    return pl.pallas_call(
        kern,
        grid=(t // tm,),
        in_specs=[
            pl.BlockSpec((tm, width), lambda i: (i, 0)),
            pl.BlockSpec((POOL_HALO, width), lambda i: (jnp.maximum(i * halo_blocks - 1, 0), 0)),
            pl.BlockSpec(w.shape, lambda i: (0, 0, 0)),
            pl.BlockSpec((1, width), lambda i: (0, 0)),
        ],
        out_specs=pl.BlockSpec((tm, width), lambda i: (i, 0)),
        out_shape=jax.ShapeDtypeStruct((t, width), BF16),
        scratch_shapes=[pltpu.VMEM((tm + POOL_HALO, width), F32)],
        compiler_params=_params(("parallel",)),
        name="pool",
    )(proj, proj, w, scale)


def _dil_kernel(q_ref, k_ref, v_ref, o_ref, m_ref, l_ref, acc_ref, *, seq):
    g = pl.program_id(2)
    qb = DIL_QBLOCK

    @pl.when(g == 0)
    def _():
        m_ref[...] = jnp.full(m_ref.shape, MASK_VALUE, F32)
        l_ref[...] = jnp.zeros(l_ref.shape, F32)
        acc_ref[...] = jnp.zeros(acc_ref.shape, F32)

    row = lax.broadcasted_iota(jnp.int32, (qb, qb), 0)
    col = lax.broadcasted_iota(jnp.int32, (qb, qb), 1)
    mask_prev = col >= row
    mask_cur = col <= row

    def run_group(dil):
        nblk = seq // dil // qb
        total = seq // qb

        def rows_at(start):
            return pl.ds(start, qb, stride=dil) if dil > 1 else pl.ds(start, qb)

        assert nblk & (nblk - 1) == 0
        shift = nblk.bit_length() - 1

        def body(idx, carry):
            r = lax.shift_right_logical(idx, shift)
            n = jnp.bitwise_and(idx, nblk - 1)
            cur = rows_at(r + dil * qb * n)
            prev = rows_at(r + dil * qb * jnp.maximum(n - 1, 0))
            q = q_ref[cur, :].astype(BF16)
            s_cur = _dot_nt(q, k_ref[cur, :].astype(BF16))
            s_prev = _dot_nt(q, k_ref[prev, :].astype(BF16))
            s_cur = jnp.where(mask_cur, s_cur, MASK_VALUE)
            s_prev = jnp.where(jnp.logical_and(mask_prev, n > 0), s_prev, MASK_VALUE)
            m_old = m_ref[cur, :]
            m_blk = jnp.maximum(jnp.max(s_cur, axis=-1, keepdims=True), jnp.max(s_prev, axis=-1, keepdims=True))
            m_new = jnp.maximum(m_old, m_blk)
            p_cur = jnp.exp(s_cur - m_new)
            p_prev = jnp.exp(s_prev - m_new)
            alpha = jnp.exp(m_old - m_new)
            l_ref[cur, :] = (alpha * l_ref[cur, :] + jnp.sum(p_cur, axis=-1, keepdims=True)
                             + jnp.sum(p_prev, axis=-1, keepdims=True))
            pv = _dot(p_cur.astype(BF16), v_ref[cur, :].astype(BF16)) + _dot(
                p_prev.astype(BF16), v_ref[prev, :].astype(BF16))
            acc_ref[cur, :] = alpha * acc_ref[cur, :] + pv
            m_ref[cur, :] = m_new
            return carry

        lax.fori_loop(0, total, body, 0, unroll=2)

    for gi, (_, dil) in enumerate(DIL_PATTERNS):
        pl.when(g == gi)(functools.partial(run_group, dil))

    @pl.when(g == len(DIL_PATTERNS) - 1)
    def _():
        o_ref[...] = (acc_ref[...] / l_ref[...]).astype(o_ref.dtype)


def _dilated(proj, *, batch, seq, col0):
    n_groups = len(DIL_PATTERNS)
    per = n_groups * ATT_HEADS
    c0 = col0 // HEAD_DIM
    kern = functools.partial(_dil_kernel, seq=seq)

    def spec(which):
        return pl.BlockSpec((seq, HEAD_DIM), lambda b, h, g: (b, c0 + which * per + g * ATT_HEADS + h))

    return pl.pallas_call(
        kern,
        grid=(batch, ATT_HEADS, n_groups),
        in_specs=[spec(0), spec(1), spec(2)],
        out_specs=pl.BlockSpec((seq, HEAD_DIM), lambda b, h, g: (b, h)),
        out_shape=jax.ShapeDtypeStruct((batch * seq, ATT_HEADS * HEAD_DIM), BF16),
        scratch_shapes=[pltpu.VMEM((seq, HEAD_DIM), F32)] * 3,
        compiler_params=_params(("parallel", "parallel", "arbitrary")),
        name="dilated",
    )(proj, proj, proj)


def _outproj_kernel(a_ref, b_ref, wa_ref, wb_ref, x_ref, g_ref, o_ref):
    y = _dot(a_ref[...], wa_ref[...]) + _dot(b_ref[...], wb_ref[...])
    o_ref[...] = x_ref[...] + _rms_norm(y, g_ref[...])


def _outproj(a, b, w, x, gain, *, tm=256):
    t, d = x.shape
    half = a.shape[1]
    assert w.shape == (2 * half, d) and t % tm == 0
    return pl.pallas_call(
        _outproj_kernel,
        grid=(t // tm,),
        in_specs=[
            pl.BlockSpec((tm, half), lambda i: (i, 0)),
            pl.BlockSpec((tm, half), lambda i: (i, 0)),
            pl.BlockSpec((half, d), lambda i: (0, 0)),
            pl.BlockSpec((half, d), lambda i: (1, 0)),
            pl.BlockSpec((tm, d), lambda i: (i, 0)),
            pl.BlockSpec((1, d), lambda i: (0, 0)),
        ],
        out_specs=pl.BlockSpec((tm, d), lambda i: (i, 0)),
        out_shape=jax.ShapeDtypeStruct((t, d), F32),
        compiler_params=_params(("parallel",)),
        name="outproj",
    )(a, b, w, w, x, gain)


def _ffn_kernel(x_ref, g_in_ref, g_out_ref, wg_ref, wu_ref, wd_ref, o_ref, hn_ref, acc_ref):
    j = pl.program_id(1)

    @pl.when(j == 0)
    def _():
        hn_ref[...] = _rms_norm(x_ref[...], g_in_ref[...]).astype(BF16)
        acc_ref[...] = jnp.zeros(acc_ref.shape, F32)

    h = hn_ref[...]
    gate = _dot(h, wg_ref[...])
    up = _dot(h, wu_ref[...])
    act = (gate * jax.nn.sigmoid(gate)) * up
    acc_ref[...] += _dot(act.astype(BF16), wd_ref[...])

    @pl.when(j == pl.num_programs(1) - 1)
    def _():
        o_ref[...] = x_ref[...] + _rms_norm(acc_ref[...], g_out_ref[...])


def _ffn(x, g_in, g_out, wg, wu, wd, *, tm=512, tf=512):
    t, d = x.shape
    f = wg.shape[1]
    assert t % tm == 0 and f % tf == 0
    return pl.pallas_call(
        _ffn_kernel,
        grid=(t // tm, f // tf),
        in_specs=[
            pl.BlockSpec((tm, d), lambda i, j: (i, 0)),
            pl.BlockSpec((1, d), lambda i, j: (0, 0)),
            pl.BlockSpec((1, d), lambda i, j: (0, 0)),
            pl.BlockSpec((d, tf), lambda i, j: (0, j)),
            pl.BlockSpec((d, tf), lambda i, j: (0, j)),
            pl.BlockSpec((tf, d), lambda i, j: (j, 0)),
        ],
        out_specs=pl.BlockSpec((tm, d), lambda i, j: (i, 0)),
        out_shape=jax.ShapeDtypeStruct((t, d), F32),
        scratch_shapes=[pltpu.VMEM((tm, d), BF16), pltpu.VMEM((tm, d), F32)],
        compiler_params=_params(("parallel", "arbitrary")),
        name="ffn",
    )(x, g_in, g_out, wg, wu, wd)


def _sb_tile(q, k, v, tri, carry, mask):
    z = _dot_nt(q, k)
    soft = jnp.maximum(z, 0.0) + jnp.log(1.0 + jnp.exp(-jnp.abs(z)))
    lg = -soft
    if mask is not None:
        lg = jnp.where(mask, lg, 0.0)
    hi = lg.astype(BF16)
    lo = (lg - hi.astype(F32)).astype(BF16)
    suffix = _dot(hi, tri) + _dot(lo, tri)
    a = jnp.exp(z + suffix + carry)
    if mask is not None:
        a = jnp.where(mask, a, 0.0)
    return _dot(a.astype(BF16), v), carry + jnp.sum(lg, axis=-1, keepdims=True)


def _sb_kernel(q_ref, k_ref, v_ref, tri_ref, o_ref, *, heads):
    qi = pl.program_id(2)
    tb = SB_BLOCK
    tri = tri_ref[...]
    row = lax.broadcasted_iota(jnp.int32, (tb, tb), 0)
    col = lax.broadcasted_iota(jnp.int32, (tb, tb), 1)
    strictly_past = col < row

    def head_cols(h):
        return slice(h * HEAD_DIM, (h + 1) * HEAD_DIM)

    qs = [q_ref[:, head_cols(h)] for h in range(heads)]
    diag = pl.ds(pl.multiple_of(qi * tb, tb), tb)
    state = []
    for h in range(heads):
        acc, carry = _sb_tile(qs[h], k_ref[diag, head_cols(h)], v_ref[diag, head_cols(h)], tri,
                              jnp.zeros((tb, 1), F32), strictly_past)
        state += [acc, carry]

    def body(it, st):
        rows = pl.ds(pl.multiple_of((qi - 1 - it) * tb, tb), tb)
        out = []
        for h in range(heads):
            pv, carry = _sb_tile(qs[h], k_ref[rows, head_cols(h)], v_ref[rows, head_cols(h)], tri,
                                 st[2 * h + 1], None)
            out += [st[2 * h] + pv, carry]
        return tuple(out)

    state = lax.fori_loop(0, qi, body, tuple(state))
    for h in range(heads):
        o_ref[:, head_cols(h)] = state[2 * h].astype(o_ref.dtype)


def _stick_breaking(proj, *, batch, seq, col0, heads_per_step=2):
    tb = SB_BLOCK
    width = heads_per_step * HEAD_DIM
    c0 = col0 // width
    per = ATT_HEADS // heads_per_step
    nq = seq // tb
    tri = (lax.broadcasted_iota(jnp.int32, (tb, tb), 0) >= lax.broadcasted_iota(jnp.int32, (tb, tb), 1)).astype(BF16)
    kern = functools.partial(_sb_kernel, heads=heads_per_step)
    return pl.pallas_call(
        kern,
        grid=(batch, per, nq),
        in_specs=[
            pl.BlockSpec((tb, width), lambda b, h, i: (b * nq + i, c0 + h)),
            pl.BlockSpec((seq, width), lambda b, h, i: (b, c0 + per + h)),
            pl.BlockSpec((seq, width), lambda b, h, i: (b, c0 + 2 * per + h)),
            pl.BlockSpec((tb, tb), lambda b, h, i: (0, 0)),
        ],
        out_specs=pl.BlockSpec((tb, width), lambda b, h, i: (b * nq + i, h)),
        out_shape=jax.ShapeDtypeStruct((batch * seq, ATT_HEADS * HEAD_DIM), BF16),
        compiler_params=_params(("parallel", "parallel", "arbitrary")),
        name="stick_breaking",
    )(proj, proj, proj, tri)


def _moba_kernel(q_ref, k_ref, v_ref, o_ref, kaug_ref, kmean_ref, qaug_ref, *, seq):
    qi = pl.program_id(2)
    blk = MOBA_BLOCK
    nb = kmean_ref.shape[1]

    @pl.when(qi == 0)
    def _():
        lane = lax.broadcasted_iota(jnp.int32, (blk, HEAD_DIM), 1)
        for n in range(seq // blk):
            kaug_ref[n * blk:(n + 1) * blk, :HEAD_DIM] = k_ref[n * blk:(n + 1) * blk, :]
            kaug_ref[n * blk:(n + 1) * blk, HEAD_DIM:] = jnp.where(lane == n, 1.0, 0.0).astype(BF16)
        key = lax.broadcasted_iota(jnp.int32, (nb, seq), 1)
        first = lax.broadcasted_iota(jnp.int32, (nb, seq), 0) * blk
        in_block = jnp.logical_and(key >= first, key < first + blk)
        mean = _dot(jnp.where(in_block, 1.0 / blk, 0.0).astype(BF16), k_ref[...])
        hi = mean.astype(BF16)
        mid = (mean - hi.astype(F32)).astype(BF16)
        lo = (mean - hi.astype(F32) - mid.astype(F32)).astype(BF16)
        kmean_ref[0] = hi
        kmean_ref[1] = mid
        kmean_ref[2] = lo

    q = q_ref[...]
    gate = _dot_nt(kmean_ref[0], q) + _dot_nt(kmean_ref[1], q) + _dot_nt(kmean_ref[2], q)
    blk_id = lax.broadcasted_iota(jnp.int32, (nb, blk), 0)
    past = blk_id < qi
    gate = jnp.where(past, gate, -jnp.inf)
    beaten = jnp.zeros((nb, blk), F32)
    for m in range(nb):
        gm = gate[m:m + 1, :]
        ahead = jnp.logical_or(gm > gate, jnp.logical_and(gm == gate, blk_id > m))
        beaten = beaten + jnp.where(ahead, 1.0, 0.0)
    keep = jnp.logical_or(jnp.logical_and(past, beaten < MOBA_TOPK), blk_id == qi)
    bias = jnp.where(keep, 0.0, MOBA_UNSELECTED)
    bias = jnp.concatenate([bias, jnp.zeros((HEAD_DIM - nb, blk), F32)], axis=0)
    qaug_ref[:, :HEAD_DIM] = q
    qaug_ref[:, HEAD_DIM:] = bias.T.astype(BF16)
    qa = qaug_ref[...]

    row = lax.broadcasted_iota(jnp.int32, (blk, blk), 0)
    col = lax.broadcasted_iota(jnp.int32, (blk, blk), 1)
    own = pl.ds(pl.multiple_of(qi * blk, blk), blk)
    s = jnp.where(col <= row, _dot_nt(qa, kaug_ref[own, :]), MASK_VALUE)
    m0 = jnp.max(s, axis=-1, keepdims=True)
    p = jnp.exp(s - m0)
    l0 = jnp.sum(p, axis=-1, keepdims=True)
    acc0 = _dot(p.astype(BF16), v_ref[own, :])

    def body(n, st):
        m_old, l_old, acc = st
        rows = pl.ds(pl.multiple_of(n * blk, blk), blk)
        s = _dot_nt(qa, kaug_ref[rows, :])
        m_new = jnp.maximum(m_old, jnp.max(s, axis=-1, keepdims=True))
        alpha = jnp.exp(m_old - m_new)
        p = jnp.exp(s - m_new)
        l_new = alpha * l_old + jnp.sum(p, axis=-1, keepdims=True)
        return m_new, l_new, alpha * acc + _dot(p.astype(BF16), v_ref[rows, :])

    _, l_fin, acc = lax.fori_loop(0, qi, body, (m0, l0, acc0))
    o_ref[...] = (acc / l_fin).astype(o_ref.dtype)


def _moba(proj, *, batch, seq, col0):
    blk = MOBA_BLOCK
    nq = seq // blk
    nb_pad = -(-nq // 16) * 16
    assert nb_pad <= HEAD_DIM
    c0 = col0 // HEAD_DIM
    kern = functools.partial(_moba_kernel, seq=seq)
    return pl.pallas_call(
        kern,
        grid=(batch, ATT_HEADS, nq),
        in_specs=[
            pl.BlockSpec((blk, HEAD_DIM), lambda b, h, i: (b * nq + i, c0 + h)),
            pl.BlockSpec((seq, HEAD_DIM), lambda b, h, i: (b, c0 + ATT_HEADS + h)),
            pl.BlockSpec((seq, HEAD_DIM), lambda b, h, i: (b, c0 + 2 * ATT_HEADS + h)),
        ],
        out_specs=pl.BlockSpec((blk, HEAD_DIM), lambda b, h, i: (b * nq + i, h)),
        out_shape=jax.ShapeDtypeStruct((batch * seq, ATT_HEADS * HEAD_DIM), BF16),
        scratch_shapes=[
            pltpu.VMEM((seq, 2 * HEAD_DIM), BF16),
            pltpu.VMEM((3, nb_pad, HEAD_DIM), BF16),
            pltpu.VMEM((blk, 2 * HEAD_DIM), BF16),
        ],
        compiler_params=_params(("parallel", "parallel", "arbitrary")),
        name="moba",
    )(proj, proj, proj)


def _rope_tables(seq):
    inv = 1.0 / (ROPE_THETA ** (jnp.arange(0, HEAD_DIM, 2, dtype=F32) / HEAD_DIM))
    ang = jnp.arange(seq, dtype=F32)[:, None] * inv[None, :]
    cos, sin = jnp.cos(ang), jnp.sin(ang)
    return jnp.concatenate([cos, cos], axis=-1), jnp.concatenate([-sin, sin], axis=-1)


def _colscale(n, q_ranges):
    cs = jnp.ones((1, n), F32)
    for lo, hi in q_ranges:
        cs = cs.at[:, lo:hi].set(HEAD_DIM ** -0.5)
    return cs


def kernel(x, norm_gains, w_in_ab, pool_w, pool_scale, w_out_ab, w_in_cd, w_out_cd, ffn_gate, ffn_up, ffn_down):
    batch, seq, d = x.shape
    att = ATT_HEADS * HEAD_DIM
    pool_width = len(POOL_WINDOWS) * POOL_GROUP
    n_groups = len(DIL_PATTERNS)
    cos, sin = _rope_tables(seq)
    gains = norm_gains.reshape(norm_gains.shape[0], 4, 1, d)
    xs = x.reshape(batch * seq, d)

    qkv0 = pool_width
    proj = _inproj(xs, gains[0, 0], w_in_ab[0].astype(BF16),
                   _colscale(w_in_ab.shape[2], [(qkv0, qkv0 + n_groups * att)]), cos, sin, seq=seq,
                   rope_cols=(qkv0, qkv0 + 2 * n_groups * att), out_dtype=F32)
    a_out = _pool(proj, pool_w[0].astype(BF16), pool_scale[0].reshape(1, pool_width), seq=seq)
    b_out = _dilated(proj, batch=batch, seq=seq, col0=qkv0)
    xs = _outproj(a_out, b_out, w_out_ab[0].astype(BF16), xs, gains[0, 1])
    xs = _ffn(xs, gains[0, 2], gains[0, 3], ffn_gate[0].astype(BF16), ffn_up[0].astype(BF16),
              ffn_down[0].astype(BF16))

    moba0 = 3 * att
    proj = _inproj(xs, gains[1, 0], w_in_cd[0].astype(BF16),
                   _colscale(w_in_cd.shape[2], [(0, att), (moba0, moba0 + att)]), cos, sin, seq=seq,
                   rope_cols=(moba0, moba0 + 2 * att), out_dtype=BF16)
    c_out = _stick_breaking(proj, batch=batch, seq=seq, col0=0)
    d_out = _moba(proj, batch=batch, seq=seq, col0=moba0)
    xs = _outproj(c_out, d_out, w_out_cd[0].astype(BF16), xs, gains[1, 1])
    xs = _ffn(xs, gains[1, 2], gains[1, 3], ffn_gate[1].astype(BF16), ffn_up[1].astype(BF16),
              ffn_down[1].astype(BF16))
    return xs.reshape(batch, seq, d)
```

```python
import functools

import jax
import jax.numpy as jnp
from jax import lax
from jax.experimental import pallas as pl
from jax.experimental.pallas import tpu as pltpu

F32 = jnp.float32
BF16 = jnp.bfloat16

HEAD_DIM = 128
EPS = 1e-6
ROPE_THETA = 10000.0
POOL_WINDOWS = (2, 4, 8, 16)
POOL_GROUP = 256
POOL_HALO = 16
DIL_PATTERNS = ((128, 1), (512, 4), (2048, 16))
DIL_QBLOCK = 128
DIL_UNROLL = 4
ATT_HEADS = 8
MOBA_BLOCK = 256
MOBA_TOPK = 3
SB_BLOCK = 256
SB_TILE = 2 * SB_BLOCK
LOG2E = 1.4426950408889634
Q_SCALE = LOG2E * HEAD_DIM ** -0.5
MASK_VALUE = -1e30
MOBA_UNSELECTED = -30000.0

V7X_VMEM_BYTES = 64 * 1024 * 1024
VMEM_LIMIT = V7X_VMEM_BYTES - 8 * 1024 * 1024


def _params(semantics):
    return pltpu.CompilerParams(dimension_semantics=semantics, vmem_limit_bytes=VMEM_LIMIT)


def _dot(a, b):
    return jnp.dot(a, b, preferred_element_type=F32)


def _dot_nt(a, b):
    return lax.dot_general(a, b, (((1,), (1,)), ((), ())), preferred_element_type=F32)


def _rms_norm(x, gain):
    return x * lax.rsqrt(jnp.mean(x * x, axis=-1, keepdims=True) + EPS) * gain


def _inproj_kernel(x_ref, g_ref, w_ref, cs_ref, cos_ref, sin_ref, o_ref, hn_ref, *, rope_lo, rope_hi):
    j = pl.program_id(1)

    @pl.when(j == 0)
    def _():
        hn_ref[...] = _rms_norm(x_ref[...], g_ref[...]).astype(BF16)

    acc = _dot(hn_ref[...], w_ref[...]) * cs_ref[...]
    is_rope = jnp.logical_and(j >= rope_lo, j < rope_hi)

    @pl.when(is_rope)
    def _():
        cos = cos_ref[...]
        sin = sin_ref[...]
        for c in range(acc.shape[1] // HEAD_DIM):
            blk = acc[:, c * HEAD_DIM:(c + 1) * HEAD_DIM]
            rot = pltpu.roll(blk, HEAD_DIM // 2, 1)
            o_ref[:, c * HEAD_DIM:(c + 1) * HEAD_DIM] = (blk * cos + rot * sin).astype(o_ref.dtype)

    @pl.when(jnp.logical_not(is_rope))
    def _():
        o_ref[...] = acc.astype(o_ref.dtype)


def _inproj(x, gain, w, colscale, cos, sin, *, seq, rope_cols, out_dtype, tm=512, tn=1024):
    t, d = x.shape
    n = w.shape[1]
    assert t % tm == 0 and n % tn == 0 and seq % tm == 0
    assert rope_cols[0] % tn == 0 and rope_cols[1] % tn == 0
    pos_blocks = seq // tm
    kern = functools.partial(_inproj_kernel, rope_lo=rope_cols[0] // tn, rope_hi=rope_cols[1] // tn)
    return pl.pallas_call(
        kern,
        grid=(t // tm, n // tn),
        in_specs=[
            pl.BlockSpec((tm, d), lambda i, j: (i, 0)),
            pl.BlockSpec((1, d), lambda i, j: (0, 0)),
            pl.BlockSpec((d, tn), lambda i, j: (0, j)),
            pl.BlockSpec((1, tn), lambda i, j: (0, j)),
            pl.BlockSpec((tm, HEAD_DIM), lambda i, j: (i % pos_blocks, 0)),
            pl.BlockSpec((tm, HEAD_DIM), lambda i, j: (i % pos_blocks, 0)),
        ],
        out_specs=pl.BlockSpec((tm, tn), lambda i, j: (i, j)),
        out_shape=jax.ShapeDtypeStruct((t, n), out_dtype),
        scratch_shapes=[pltpu.VMEM((tm, d), BF16)],
        compiler_params=_params(("parallel", "arbitrary")),
        name="inproj",
    )(x, gain, w, colscale, cos, sin)


def _pool_kernel(u_ref, halo_ref, w_ref, sc_ref, o_ref, ext_ref, *, tm, seq):
    pos0 = (pl.program_id(0) * tm) % seq
    ext_ref[pl.ds(POOL_HALO, tm), :] = u_ref[...]
    ext_ref[pl.ds(0, POOL_HALO), :] = jnp.where(pos0 == 0, 0.0, halo_ref[...])
    pos = pos0 + lax.broadcasted_iota(jnp.int32, (tm, 1), 0)
    for g, w in enumerate(POOL_WINDOWS):
        cols = pl.ds(g * POOL_GROUP, POOL_GROUP)
        cur = ext_ref[pl.ds(POOL_HALO, tm), cols]
        tot = cur
        for back in range(1, w):
            tot = tot + ext_ref[pl.ds(POOL_HALO - back, tm), cols]
        cnt = jnp.minimum(pos + 1, w).astype(F32)
        pooled = tot / cnt - cur
        y = _dot(pooled.astype(BF16), w_ref[g])
        o_ref[:, g * POOL_GROUP:(g + 1) * POOL_GROUP] = (
            y * sc_ref[:, g * POOL_GROUP:(g + 1) * POOL_GROUP]).astype(o_ref.dtype)


def _pool(proj, w, scale, *, seq, tm=512):
    t = proj.shape[0]
    width = len(POOL_WINDOWS) * POOL_GROUP
    assert seq % tm == 0 and tm % POOL_HALO == 0
    halo_blocks = tm // POOL_HALO
    kern = functools.partial(_pool_kernel, tm=tm, seq=seq)
    return pl.pallas_call(
        kern,
        grid=(t // tm,),
        in_specs=[
            pl.BlockSpec((tm, width), lambda i: (i, 0)),
            pl.BlockSpec((POOL_HALO, width), lambda i: (jnp.maximum(i * halo_blocks - 1, 0), 0)),
            pl.BlockSpec(w.shape, lambda i: (0, 0, 0)),
            pl.BlockSpec((1, width), lambda i: (0, 0)),
        ],
        out_specs=pl.BlockSpec((tm, width), lambda i: (i, 0)),
        out_shape=jax.ShapeDtypeStruct((t, width), BF16),
        scratch_shapes=[pltpu.VMEM((tm + POOL_HALO, width), F32)],
        compiler_params=_params(("parallel",)),
        name="pool",
    )(proj, proj, w, scale)


def _dil_kernel(q_ref, k_ref, v_ref, o_ref, qd_ref, kd_ref, vd_ref, m_ref, l_ref, acc_ref, *, seq):
    g = pl.program_id(2)
    qb = DIL_QBLOCK
    nblocks = seq // qb

    @pl.when(g == 0)
    def _():
        m_ref[...] = jnp.full(m_ref.shape, MASK_VALUE, F32)
        l_ref[...] = jnp.zeros(l_ref.shape, F32)
        acc_ref[...] = jnp.zeros(acc_ref.shape, F32)
        kd_ref[pl.ds(0, qb), :] = jnp.zeros((qb, HEAD_DIM), BF16)
        vd_ref[pl.ds(0, qb), :] = jnp.zeros((qb, HEAD_DIM), BF16)

    row = lax.broadcasted_iota(jnp.int32, (qb, 2 * qb), 0)
    col = lax.broadcasted_iota(jnp.int32, (qb, 2 * qb), 1)
    band = jnp.logical_and(col >= row, col <= row + qb)
    own_block = col >= qb

    def run_group(dil):
        nblk = seq // dil // qb
        assert nblk & (nblk - 1) == 0
        shift = nblk.bit_length() - 1

        def positions(idx):
            r = lax.shift_right_logical(idx, shift)
            n = jnp.bitwise_and(idx, nblk - 1)
            start = r + dil * qb * n
            return (pl.ds(start, qb, stride=dil) if dil > 1 else pl.ds(pl.multiple_of(start, qb), qb)), n

        def gather(idx, carry):
            src, _ = positions(idx)
            qd_ref[pl.ds(pl.multiple_of(idx * qb, qb), qb), :] = q_ref[src, :].astype(BF16)
            dst = pl.ds(pl.multiple_of((idx + 1) * qb, qb), qb)
            kd_ref[dst, :] = k_ref[src, :].astype(BF16)
            vd_ref[dst, :] = v_ref[src, :].astype(BF16)
            return carry

        lax.fori_loop(0, nblocks, gather, 0, unroll=4)

        def block(idx):
            cur, n = positions(idx)
            q = qd_ref[pl.ds(pl.multiple_of(idx * qb, qb), qb), :]
            win = pl.ds(pl.multiple_of(idx * qb, qb), 2 * qb)
            s = _dot_nt(q, kd_ref[win, :])
            s = jnp.where(jnp.logical_and(band, jnp.logical_or(own_block, n > 0)), s, MASK_VALUE)
            m_old = m_ref[cur, :]
            m_new = jnp.maximum(m_old, jnp.max(s, axis=-1, keepdims=True))
            p = jnp.exp2(s - jnp.concatenate([m_new, m_new], axis=1))
            alpha = jnp.exp2(m_old - m_new)
            l_ref[cur, :] = alpha * l_ref[cur, :] + jnp.sum(p, axis=-1, keepdims=True)
            acc_ref[cur, :] = alpha * acc_ref[cur, :] + _dot(p.astype(BF16), vd_ref[win, :])
            m_ref[cur, :] = m_new

        def body(it, carry):
            for u in range(DIL_UNROLL):
                block(it * DIL_UNROLL + u)
            return carry

        lax.fori_loop(0, nblocks // DIL_UNROLL, body, 0)

    for gi, (_, dil) in enumerate(DIL_PATTERNS):
        pl.when(g == gi)(functools.partial(run_group, dil))

    @pl.when(g == len(DIL_PATTERNS) - 1)
    def _():
        o_ref[...] = (acc_ref[...] / l_ref[...]).astype(o_ref.dtype)


def _dilated(proj, *, batch, seq, col0):
    n_groups = len(DIL_PATTERNS)
    per = n_groups * ATT_HEADS
    c0 = col0 // HEAD_DIM
    kern = functools.partial(_dil_kernel, seq=seq)

    def spec(which):
        return pl.BlockSpec((seq, HEAD_DIM), lambda b, h, g: (b, c0 + which * per + g * ATT_HEADS + h))

    return pl.pallas_call(
        kern,
        grid=(batch, ATT_HEADS, n_groups),
        in_specs=[spec(0), spec(1), spec(2)],
        out_specs=pl.BlockSpec((seq, HEAD_DIM), lambda b, h, g: (b, h)),
        out_shape=jax.ShapeDtypeStruct((batch * seq, ATT_HEADS * HEAD_DIM), BF16),
        scratch_shapes=[
            pltpu.VMEM((seq, HEAD_DIM), BF16),
            pltpu.VMEM((seq + DIL_QBLOCK, HEAD_DIM), BF16),
            pltpu.VMEM((seq + DIL_QBLOCK, HEAD_DIM), BF16),
        ] + [pltpu.VMEM((seq, HEAD_DIM), F32)] * 3,
        compiler_params=_params(("parallel", "parallel", "arbitrary")),
        name="dilated",
    )(proj, proj, proj)


def _outproj_kernel(a_ref, b_ref, wa_ref, wb_ref, x_ref, g_ref, o_ref):
    y = _dot(a_ref[...], wa_ref[...]) + _dot(b_ref[...], wb_ref[...])
    o_ref[...] = x_ref[...] + _rms_norm(y, g_ref[...])


def _outproj(a, b, w, x, gain, *, tm=256):
    t, d = x.shape
    half = a.shape[1]
    assert w.shape == (2 * half, d) and t % tm == 0
    return pl.pallas_call(
        _outproj_kernel,
        grid=(t // tm,),
        in_specs=[
            pl.BlockSpec((tm, half), lambda i: (i, 0)),
            pl.BlockSpec((tm, half), lambda i: (i, 0)),
            pl.BlockSpec((half, d), lambda i: (0, 0)),
            pl.BlockSpec((half, d), lambda i: (1, 0)),
            pl.BlockSpec((tm, d), lambda i: (i, 0)),
            pl.BlockSpec((1, d), lambda i: (0, 0)),
        ],
        out_specs=pl.BlockSpec((tm, d), lambda i: (i, 0)),
        out_shape=jax.ShapeDtypeStruct((t, d), F32),
        compiler_params=_params(("parallel",)),
        name="outproj",
    )(a, b, w, w, x, gain)


def _ffn_kernel(x_ref, g_in_ref, g_out_ref, wg_ref, wu_ref, wd_ref, o_ref, hn_ref, acc_ref):
    j = pl.program_id(1)

    @pl.when(j == 0)
    def _():
        hn_ref[...] = _rms_norm(x_ref[...], g_in_ref[...]).astype(BF16)
        acc_ref[...] = jnp.zeros(acc_ref.shape, F32)

    h = hn_ref[...]
    gate = _dot(h, wg_ref[...])
    up = _dot(h, wu_ref[...])
    act = (gate * jax.nn.sigmoid(gate)) * up
    acc_ref[...] += _dot(act.astype(BF16), wd_ref[...])

    @pl.when(j == pl.num_programs(1) - 1)
    def _():
        o_ref[...] = x_ref[...] + _rms_norm(acc_ref[...], g_out_ref[...])


def _ffn(x, g_in, g_out, wg, wu, wd, *, tm=512, tf=512):
    t, d = x.shape
    f = wg.shape[1]
    assert t % tm == 0 and f % tf == 0
    return pl.pallas_call(
        _ffn_kernel,
        grid=(t // tm, f // tf),
        in_specs=[
            pl.BlockSpec((tm, d), lambda i, j: (i, 0)),
            pl.BlockSpec((1, d), lambda i, j: (0, 0)),
            pl.BlockSpec((1, d), lambda i, j: (0, 0)),
            pl.BlockSpec((d, tf), lambda i, j: (0, j)),
            pl.BlockSpec((d, tf), lambda i, j: (0, j)),
            pl.BlockSpec((tf, d), lambda i, j: (j, 0)),
        ],
        out_specs=pl.BlockSpec((tm, d), lambda i, j: (i, 0)),
        out_shape=jax.ShapeDtypeStruct((t, d), F32),
        scratch_shapes=[pltpu.VMEM((tm, d), BF16), pltpu.VMEM((tm, d), F32)],
        compiler_params=_params(("parallel", "arbitrary")),
        name="ffn",
    )(x, g_in, g_out, wg, wu, wd)


def _sb_tile(q, k, v, tri2, carry, mask):
    tb = SB_BLOCK
    z = _dot_nt(q, k)
    soft = jnp.maximum(z, 0.0) + jnp.log(1.0 + jnp.exp2(-jnp.abs(z))) * LOG2E
    if mask is not None:
        soft = jnp.where(mask, soft, 0.0)
    hi = soft.astype(BF16)
    lo = (soft - hi.astype(F32)).astype(BF16)
    left, right = slice(0, tb), slice(tb, 2 * tb)
    suffix_l = _dot(jnp.concatenate([hi[:, left], lo[:, left]], axis=1), tri2)
    suffix_r = _dot(jnp.concatenate([hi[:, right], lo[:, right]], axis=1), tri2)
    total_l = jnp.sum(soft[:, left], axis=-1, keepdims=True)
    total_r = jnp.sum(soft[:, right], axis=-1, keepdims=True)
    expo = jnp.concatenate([z[:, left] - suffix_l - (carry + total_r), z[:, right] - suffix_r - carry], axis=1)
    a = jnp.exp2(expo)
    if mask is not None:
        a = jnp.where(mask, a, 0.0)
    return _dot(a.astype(BF16), v), carry + total_r + total_l


def _sb_kernel(q_ref, k_ref, v_ref, tri_ref, o_ref, *, heads):
    qi = pl.program_id(2)
    tq = SB_TILE
    tri2 = tri_ref[...]
    row = lax.broadcasted_iota(jnp.int32, (tq, tq), 0)
    col = lax.broadcasted_iota(jnp.int32, (tq, tq), 1)
    strictly_past = col < row

    def head_cols(h):
        return slice(h * HEAD_DIM, (h + 1) * HEAD_DIM)

    qs = [q_ref[:, head_cols(h)] for h in range(heads)]
    diag = pl.ds(pl.multiple_of(qi * tq, tq), tq)
    state = []
    for h in range(heads):
        acc, carry = _sb_tile(qs[h], k_ref[diag, head_cols(h)], v_ref[diag, head_cols(h)], tri2,
                              jnp.zeros((tq, 1), F32), strictly_past)
        state += [acc, carry]

    def body(it, st):
        rows = pl.ds(pl.multiple_of((qi - 1 - it) * tq, tq), tq)
        out = []
        for h in range(heads):
            pv, carry = _sb_tile(qs[h], k_ref[rows, head_cols(h)], v_ref[rows, head_cols(h)], tri2,
                                 st[2 * h + 1], None)
            out += [st[2 * h] + pv, carry]
        return tuple(out)

    state = lax.fori_loop(0, qi, body, tuple(state))
    for h in range(heads):
        o_ref[:, head_cols(h)] = state[2 * h].astype(o_ref.dtype)


def _stick_breaking(proj, *, batch, seq, col0, heads_per_step=2):
    tb, tq = SB_BLOCK, SB_TILE
    assert tq == 2 * tb and seq % tq == 0
    width = heads_per_step * HEAD_DIM
    c0 = col0 // width
    per = ATT_HEADS // heads_per_step
    nq = seq // tq
    tri = lax.broadcasted_iota(jnp.int32, (tb, tb), 0) >= lax.broadcasted_iota(jnp.int32, (tb, tb), 1)
    tri2 = jnp.concatenate([tri, tri], axis=0).astype(BF16)
    kern = functools.partial(_sb_kernel, heads=heads_per_step)
    return pl.pallas_call(
        kern,
        grid=(batch, per, nq),
        in_specs=[
            pl.BlockSpec((tq, width), lambda b, h, i: (b * nq + i, c0 + h)),
            pl.BlockSpec((seq, width), lambda b, h, i: (b, c0 + per + h)),
            pl.BlockSpec((seq, width), lambda b, h, i: (b, c0 + 2 * per + h)),
            pl.BlockSpec((2 * tb, tb), lambda b, h, i: (0, 0)),
        ],
        out_specs=pl.BlockSpec((tq, width), lambda b, h, i: (b * nq + i, h)),
        out_shape=jax.ShapeDtypeStruct((batch * seq, ATT_HEADS * HEAD_DIM), BF16),
        compiler_params=_params(("parallel", "parallel", "arbitrary")),
        name="stick_breaking",
    )(proj, proj, proj, tri2)


def _moba_setup(k, kaug_ref, kmean_ref, *, seq):
    blk = MOBA_BLOCK
    nb = kmean_ref.shape[1]
    lane = lax.broadcasted_iota(jnp.int32, (blk, HEAD_DIM), 1)
    for n in range(seq // blk):
        kaug_ref[n * blk:(n + 1) * blk, :HEAD_DIM] = k[n * blk:(n + 1) * blk, :]
        kaug_ref[n * blk:(n + 1) * blk, HEAD_DIM:] = jnp.where(lane == n, 1.0, 0.0).astype(BF16)
    key = lax.broadcasted_iota(jnp.int32, (nb, seq), 1)
    first = lax.broadcasted_iota(jnp.int32, (nb, seq), 0) * blk
    in_block = jnp.logical_and(key >= first, key < first + blk)
    mean = _dot(jnp.where(in_block, 1.0 / blk, 0.0).astype(BF16), k)
    hi = mean.astype(BF16)
    mid = (mean - hi.astype(F32)).astype(BF16)
    lo = (mean - hi.astype(F32) - mid.astype(F32)).astype(BF16)
    kmean_ref[0] = hi
    kmean_ref[1] = mid
    kmean_ref[2] = lo


def _moba_select(q, qi, kmean_ref):
    blk = MOBA_BLOCK
    nb = kmean_ref.shape[1]
    gate = _dot_nt(kmean_ref[0], q) + _dot_nt(kmean_ref[1], q) + _dot_nt(kmean_ref[2], q)
    blk_id = lax.broadcasted_iota(jnp.int32, (nb, blk), 0)
    past = blk_id < qi
    gate = jnp.where(past, gate, -jnp.inf)
    beaten = jnp.zeros((nb, blk), F32)
    for m in range(nb):
        gm = gate[m:m + 1, :]
        ahead = jnp.logical_or(gm > gate, jnp.logical_and(gm == gate, blk_id > m))
        beaten = beaten + jnp.where(ahead, 1.0, 0.0)
    keep = jnp.logical_or(jnp.logical_and(past, beaten < MOBA_TOPK), blk_id == qi)
    bias = jnp.where(keep, 0.0, MOBA_UNSELECTED)
    bias = jnp.concatenate([bias, jnp.zeros((HEAD_DIM - nb, blk), F32)], axis=0)
    return jnp.concatenate([q, bias.T.astype(BF16)], axis=1)


def _moba_kernel(q_ref, k_ref, v_ref, o_ref, kaug_ref, kmean_ref, *, seq, heads):
    qi = pl.program_id(2)
    blk = MOBA_BLOCK
    pair = 2 * blk

    def head_cols(h):
        return slice(h * HEAD_DIM, (h + 1) * HEAD_DIM)

    @pl.when(qi == 0)
    def _():
        for h in range(heads):
            _moba_setup(k_ref[:, head_cols(h)], kaug_ref.at[h], kmean_ref.at[h], seq=seq)

    qa = [_moba_select(q_ref[:, head_cols(h)], qi, kmean_ref.at[h]) for h in range(heads)]

    def scores(h, j):
        rows = pl.ds(pl.multiple_of(j * pair, pair), pair)
        return _dot_nt(qa[h], kaug_ref[h, rows, :]), v_ref[rows, head_cols(h)]

    last = lax.shift_right_logical(qi, 1)
    row = lax.broadcasted_iota(jnp.int32, (blk, pair), 0)
    col = lax.broadcasted_iota(jnp.int32, (blk, pair), 1)
    causal = col <= row + (qi - 2 * last) * blk
    state = []
    for h in range(heads):
        s, v = scores(h, last)
        s = jnp.where(causal, s, MASK_VALUE)
        m = jnp.max(s, axis=-1, keepdims=True)
        p = jnp.exp2(s - m)
        state += [m, jnp.sum(p, axis=-1, keepdims=True), _dot(p.astype(BF16), v)]

    def body(j, st):
        out = []
        for h in range(heads):
            m_old, l_old, acc = st[3 * h:3 * h + 3]
            s, v = scores(h, j)
            m_new = jnp.maximum(m_old, jnp.max(s, axis=-1, keepdims=True))
            alpha = jnp.exp2(m_old - m_new)
            p = jnp.exp2(s - m_new)
            out += [m_new, alpha * l_old + jnp.sum(p, axis=-1, keepdims=True),
                    alpha * acc + _dot(p.astype(BF16), v)]
        return tuple(out)

    state = lax.fori_loop(0, last, body, tuple(state))
    for h in range(heads):
        o_ref[:, head_cols(h)] = (state[3 * h + 2] / state[3 * h + 1]).astype(o_ref.dtype)


def _moba(proj, *, batch, seq, col0, heads_per_step=4):
    blk = MOBA_BLOCK
    nq = seq // blk
    assert nq % 2 == 0
    nb_pad = -(-nq // 16) * 16
    assert nb_pad <= HEAD_DIM
    width = heads_per_step * HEAD_DIM
    c0 = col0 // width
    per = ATT_HEADS // heads_per_step
    kern = functools.partial(_moba_kernel, seq=seq, heads=heads_per_step)
    return pl.pallas_call(
        kern,
        grid=(batch, per, nq),
        in_specs=[
            pl.BlockSpec((blk, width), lambda b, h, i: (b * nq + i, c0 + h)),
            pl.BlockSpec((seq, width), lambda b, h, i: (b, c0 + per + h)),
            pl.BlockSpec((seq, width), lambda b, h, i: (b, c0 + 2 * per + h)),
        ],
        out_specs=pl.BlockSpec((blk, width), lambda b, h, i: (b * nq + i, h)),
        out_shape=jax.ShapeDtypeStruct((batch * seq, ATT_HEADS * HEAD_DIM), BF16),
        scratch_shapes=[
            pltpu.VMEM((heads_per_step, seq, 2 * HEAD_DIM), BF16),
            pltpu.VMEM((heads_per_step, 3, nb_pad, HEAD_DIM), BF16),
        ],
        compiler_params=_params(("parallel", "parallel", "arbitrary")),
        name="moba",
    )(proj, proj, proj)


def _rope_tables(seq):
    inv = 1.0 / (ROPE_THETA ** (jnp.arange(0, HEAD_DIM, 2, dtype=F32) / HEAD_DIM))
    ang = jnp.arange(seq, dtype=F32)[:, None] * inv[None, :]
    cos, sin = jnp.cos(ang), jnp.sin(ang)
    return jnp.concatenate([cos, cos], axis=-1), jnp.concatenate([-sin, sin], axis=-1)


def _colscale(n, q_ranges):
    cs = jnp.ones((1, n), F32)
    for lo, hi in q_ranges:
        cs = cs.at[:, lo:hi].set(Q_SCALE)
    return cs


def kernel(x, norm_gains, w_in_ab, pool_w, pool_scale, w_out_ab, w_in_cd, w_out_cd, ffn_gate, ffn_up, ffn_down):
    batch, seq, d = x.shape
    att = ATT_HEADS * HEAD_DIM
    pool_width = len(POOL_WINDOWS) * POOL_GROUP
    n_groups = len(DIL_PATTERNS)
    cos, sin = _rope_tables(seq)
    gains = norm_gains.reshape(norm_gains.shape[0], 4, 1, d)
    xs = x.reshape(batch * seq, d)

    qkv0 = pool_width
    proj = _inproj(xs, gains[0, 0], w_in_ab[0].astype(BF16),
                   _colscale(w_in_ab.shape[2], [(qkv0, qkv0 + n_groups * att)]), cos, sin, seq=seq,
                   rope_cols=(qkv0, qkv0 + 2 * n_groups * att), out_dtype=F32)
    a_out = _pool(proj, pool_w[0].astype(BF16), pool_scale[0].reshape(1, pool_width), seq=seq)
    b_out = _dilated(proj, batch=batch, seq=seq, col0=qkv0)
    xs = _outproj(a_out, b_out, w_out_ab[0].astype(BF16), xs, gains[0, 1])
    xs = _ffn(xs, gains[0, 2], gains[0, 3], ffn_gate[0].astype(BF16), ffn_up[0].astype(BF16),
              ffn_down[0].astype(BF16))

    moba0 = 3 * att
    proj = _inproj(xs, gains[1, 0], w_in_cd[0].astype(BF16),
                   _colscale(w_in_cd.shape[2], [(0, att), (moba0, moba0 + att)]), cos, sin, seq=seq,
                   rope_cols=(moba0, moba0 + 2 * att), out_dtype=BF16)
    c_out = _stick_breaking(proj, batch=batch, seq=seq, col0=0)
    d_out = _moba(proj, batch=batch, seq=seq, col0=moba0)
    xs = _outproj(c_out, d_out, w_out_cd[0].astype(BF16), xs, gains[1, 1])
    xs = _ffn(xs, gains[1, 2], gains[1, 3], ffn_gate[1].astype(BF16), ffn_up[1].astype(BF16),
              ffn_down[1].astype(BF16))
    return xs.reshape(batch, seq, d)
```

```python
import functools

import jax
import jax.numpy as jnp
from jax import lax
from jax.experimental import pallas as pl
from jax.experimental.pallas import tpu as pltpu

F32 = jnp.float32
BF16 = jnp.bfloat16

HEAD_DIM = 128
EPS = 1e-6
ROPE_THETA = 10000.0
POOL_WINDOWS = (2, 4, 8, 16)
POOL_GROUP = 256
POOL_HALO = 16
DIL_PATTERNS = ((128, 1), (512, 4), (2048, 16))
DIL_QBLOCK = 128
DIL_UNROLL = 8
ATT_HEADS = 8
MOBA_BLOCK = 256
MOBA_TOPK = 3
SB_BLOCK = 256
SB_TILE = 2 * SB_BLOCK
LOG2E = 1.4426950408889634
Q_SCALE = LOG2E * HEAD_DIM ** -0.5
MASK_VALUE = -1e30
MOBA_UNSELECTED = -30000.0

V7X_VMEM_BYTES = 64 * 1024 * 1024
VMEM_LIMIT = V7X_VMEM_BYTES - 8 * 1024 * 1024


def _params(semantics):
    return pltpu.CompilerParams(dimension_semantics=semantics, vmem_limit_bytes=VMEM_LIMIT)


def _dot(a, b):
    return jnp.dot(a, b, preferred_element_type=F32)


def _dot_nt(a, b):
    return lax.dot_general(a, b, (((1,), (1,)), ((), ())), preferred_element_type=F32)


def _rms_norm(x, gain):
    return x * lax.rsqrt(jnp.mean(x * x, axis=-1, keepdims=True) + EPS) * gain


def _inproj_kernel(x_ref, g_ref, w_ref, cs_ref, cos_ref, sin_ref, o_ref, hn_ref, *, rope_lo, rope_hi):
    j = pl.program_id(1)

    @pl.when(j == 0)
    def _():
        hn_ref[...] = _rms_norm(x_ref[...], g_ref[...]).astype(BF16)

    acc = _dot(hn_ref[...], w_ref[...]) * cs_ref[...]
    is_rope = jnp.logical_and(j >= rope_lo, j < rope_hi)
    cos = jnp.where(is_rope, cos_ref[...], 1.0)
    sin = jnp.where(is_rope, sin_ref[...], 0.0)
    for c in range(acc.shape[1] // HEAD_DIM):
        blk = acc[:, c * HEAD_DIM:(c + 1) * HEAD_DIM]
        rot = pltpu.roll(blk, HEAD_DIM // 2, 1)
        o_ref[:, c * HEAD_DIM:(c + 1) * HEAD_DIM] = (blk * cos + rot * sin).astype(o_ref.dtype)


def _inproj(x, gain, w, colscale, cos, sin, *, seq, rope_cols, tm=1024, tn=1024):
    t, d = x.shape
    n = w.shape[1]
    assert t % tm == 0 and n % tn == 0 and seq % tm == 0
    assert rope_cols[0] % tn == 0 and rope_cols[1] % tn == 0
    pos_blocks = seq // tm
    kern = functools.partial(_inproj_kernel, rope_lo=rope_cols[0] // tn, rope_hi=rope_cols[1] // tn)
    return pl.pallas_call(
        kern,
        grid=(t // tm, n // tn),
        in_specs=[
            pl.BlockSpec((tm, d), lambda i, j: (i, 0)),
            pl.BlockSpec((1, d), lambda i, j: (0, 0)),
            pl.BlockSpec((d, tn), lambda i, j: (0, j)),
            pl.BlockSpec((1, tn), lambda i, j: (0, j)),
            pl.BlockSpec((tm, HEAD_DIM), lambda i, j: (i % pos_blocks, 0)),
            pl.BlockSpec((tm, HEAD_DIM), lambda i, j: (i % pos_blocks, 0)),
        ],
        out_specs=pl.BlockSpec((tm, tn), lambda i, j: (i, j)),
        out_shape=jax.ShapeDtypeStruct((t, n), BF16),
        scratch_shapes=[pltpu.VMEM((tm, d), BF16)],
        compiler_params=_params(("parallel", "arbitrary")),
        name="inproj",
    )(x, gain, w, colscale, cos, sin)


def _pool_kernel(u_ref, halo_ref, w_ref, sc_ref, o_ref, ext_ref, *, tm, seq):
    pos0 = (pl.program_id(0) * tm) % seq
    ext_ref[pl.ds(POOL_HALO, tm), :] = u_ref[...].astype(F32)
    ext_ref[pl.ds(0, POOL_HALO), :] = jnp.where(pos0 == 0, 0.0, halo_ref[...].astype(F32))
    pos = pos0 + lax.broadcasted_iota(jnp.int32, (tm, 1), 0)
    for g, w in enumerate(POOL_WINDOWS):
        cols = pl.ds(g * POOL_GROUP, POOL_GROUP)
        cur = ext_ref[pl.ds(POOL_HALO, tm), cols]
        tot = cur
        for back in range(1, w):
            tot = tot + ext_ref[pl.ds(POOL_HALO - back, tm), cols]
        cnt = jnp.minimum(pos + 1, w).astype(F32)
        pooled = tot / cnt - cur
        y = _dot(pooled.astype(BF16), w_ref[g])
        o_ref[:, g * POOL_GROUP:(g + 1) * POOL_GROUP] = (
            y * sc_ref[:, g * POOL_GROUP:(g + 1) * POOL_GROUP]).astype(o_ref.dtype)


def _pool(proj, w, scale, *, seq, tm=512):
    t = proj.shape[0]
    width = len(POOL_WINDOWS) * POOL_GROUP
    assert seq % tm == 0 and tm % POOL_HALO == 0
    halo_blocks = tm // POOL_HALO
    kern = functools.partial(_pool_kernel, tm=tm, seq=seq)
    return pl.pallas_call(
        kern,
        grid=(t // tm,),
        in_specs=[
            pl.BlockSpec((tm, width), lambda i: (i, 0)),
            pl.BlockSpec((POOL_HALO, width), lambda i: (jnp.maximum(i * halo_blocks - 1, 0), 0)),
            pl.BlockSpec(w.shape, lambda i: (0, 0, 0)),
            pl.BlockSpec((1, width), lambda i: (0, 0)),
        ],
        out_specs=pl.BlockSpec((tm, width), lambda i: (i, 0)),
        out_shape=jax.ShapeDtypeStruct((t, width), BF16),
        scratch_shapes=[pltpu.VMEM((tm + POOL_HALO, width), F32)],
        compiler_params=_params(("parallel",)),
        name="pool",
    )(proj, proj, w, scale)


def _dil_kernel(q_ref, k_ref, v_ref, o_ref, qd_ref, kd_ref, vd_ref, qw_ref, kw_ref, vw_ref,
                m_ref, l_ref, acc_ref, *, seq):
    g = pl.program_id(2)
    wide_refs = (qw_ref, kw_ref, vw_ref)
    qb = DIL_QBLOCK
    nblocks = seq // qb

    @pl.when(g == 0)
    def _():
        m_ref[...] = jnp.full(m_ref.shape, MASK_VALUE, F32)
        l_ref[...] = jnp.zeros(l_ref.shape, F32)
        acc_ref[...] = jnp.zeros(acc_ref.shape, F32)
        kd_ref[pl.ds(0, qb), :] = jnp.zeros((qb, HEAD_DIM), BF16)
        vd_ref[pl.ds(0, qb), :] = jnp.zeros((qb, HEAD_DIM), BF16)

    row = lax.broadcasted_iota(jnp.int32, (qb, 2 * qb), 0)
    col = lax.broadcasted_iota(jnp.int32, (qb, 2 * qb), 1)
    band = jnp.logical_and(col >= row, col <= row + qb)
    own_block = col >= qb

    def run_group(dil):
        nblk = seq // dil // qb
        assert nblk & (nblk - 1) == 0
        shift = nblk.bit_length() - 1

        def positions(idx):
            r = lax.shift_right_logical(idx, shift)
            n = jnp.bitwise_and(idx, nblk - 1)
            start = r + dil * qb * n
            return (pl.ds(start, qb, stride=dil) if dil > 1 else pl.ds(pl.multiple_of(start, qb), qb)), n

        if dil > 1:
            def widen(idx, carry):
                rows = pl.ds(pl.multiple_of(idx * qb, qb), qb)
                for src_ref, wide_ref in zip((q_ref, k_ref, v_ref), wide_refs):
                    wide_ref[rows, :] = src_ref[rows, :].astype(F32)
                return carry

            lax.fori_loop(0, nblocks, widen, 0, unroll=4)
            sources = wide_refs
        else:
            sources = (q_ref, k_ref, v_ref)

        def gather(idx, carry):
            src, _ = positions(idx)
            qd_ref[pl.ds(pl.multiple_of(idx * qb, qb), qb), :] = sources[0][src, :].astype(BF16)
            dst = pl.ds(pl.multiple_of((idx + 1) * qb, qb), qb)
            kd_ref[dst, :] = sources[1][src, :].astype(BF16)
            vd_ref[dst, :] = sources[2][src, :].astype(BF16)
            return carry

        lax.fori_loop(0, nblocks, gather, 0, unroll=4)

        def window(idx):
            return pl.ds(pl.multiple_of(idx * qb, qb), 2 * qb)

        def block_scores(idx):
            q = qd_ref[pl.ds(pl.multiple_of(idx * qb, qb), qb), :]
            return _dot_nt(q, kd_ref[window(idx), :])

        def block_update(idx, s):
            cur, n = positions(idx)
            s = jnp.where(jnp.logical_and(band, jnp.logical_or(own_block, n > 0)), s, MASK_VALUE)
            m_old = m_ref[cur, :]
            m_new = jnp.maximum(m_old, jnp.max(s, axis=-1, keepdims=True))
            p = jnp.exp2(s - jnp.concatenate([m_new, m_new], axis=1))
            alpha = jnp.exp2(m_old - m_new)
            l_ref[cur, :] = alpha * l_ref[cur, :] + jnp.sum(p, axis=-1, keepdims=True)
            acc_ref[cur, :] = alpha * acc_ref[cur, :] + _dot(p.astype(BF16), vd_ref[window(idx), :])
            m_ref[cur, :] = m_new

        def body(it, carry):
            first = it * DIL_UNROLL
            scores = [block_scores(first + u) for u in range(DIL_UNROLL)]
            for u in range(DIL_UNROLL):
                block_update(first + u, scores[u])
            return carry

        lax.fori_loop(0, nblocks // DIL_UNROLL, body, 0)

    for gi, (_, dil) in enumerate(DIL_PATTERNS):
        pl.when(g == gi)(functools.partial(run_group, dil))

    @pl.when(g == len(DIL_PATTERNS) - 1)
    def _():
        o_ref[...] = (acc_ref[...] / l_ref[...]).astype(o_ref.dtype)


def _dilated(proj, *, batch, seq, col0):
    n_groups = len(DIL_PATTERNS)
    per = n_groups * ATT_HEADS
    c0 = col0 // HEAD_DIM
    kern = functools.partial(_dil_kernel, seq=seq)

    def spec(which):
        return pl.BlockSpec((seq, HEAD_DIM), lambda b, h, g: (b, c0 + which * per + g * ATT_HEADS + h))

    return pl.pallas_call(
        kern,
        grid=(batch, ATT_HEADS, n_groups),
        in_specs=[spec(0), spec(1), spec(2)],
        out_specs=pl.BlockSpec((seq, HEAD_DIM), lambda b, h, g: (b, h)),
        out_shape=jax.ShapeDtypeStruct((batch * seq, ATT_HEADS * HEAD_DIM), BF16),
        scratch_shapes=[
            pltpu.VMEM((seq, HEAD_DIM), BF16),
            pltpu.VMEM((seq + DIL_QBLOCK, HEAD_DIM), BF16),
            pltpu.VMEM((seq + DIL_QBLOCK, HEAD_DIM), BF16),
        ] + [pltpu.VMEM((seq, HEAD_DIM), F32)] * 6,
        compiler_params=_params(("parallel", "parallel", "arbitrary")),
        name="dilated",
    )(proj, proj, proj)


def _outproj_kernel(a_ref, b_ref, wa_ref, wb_ref, x_ref, g_ref, o_ref):
    y = _dot(a_ref[...], wa_ref[...]) + _dot(b_ref[...], wb_ref[...])
    o_ref[...] = x_ref[...] + _rms_norm(y, g_ref[...])


def _outproj(a, b, w, x, gain, *, tm=256):
    t, d = x.shape
    half = a.shape[1]
    assert w.shape == (2 * half, d) and t % tm == 0
    return pl.pallas_call(
        _outproj_kernel,
        grid=(t // tm,),
        in_specs=[
            pl.BlockSpec((tm, half), lambda i: (i, 0)),
            pl.BlockSpec((tm, half), lambda i: (i, 0)),
            pl.BlockSpec((half, d), lambda i: (0, 0)),
            pl.BlockSpec((half, d), lambda i: (1, 0)),
            pl.BlockSpec((tm, d), lambda i: (i, 0)),
            pl.BlockSpec((1, d), lambda i: (0, 0)),
        ],
        out_specs=pl.BlockSpec((tm, d), lambda i: (i, 0)),
        out_shape=jax.ShapeDtypeStruct((t, d), F32),
        compiler_params=_params(("parallel",)),
        name="outproj",
    )(a, b, w, w, x, gain)


def _ffn_kernel(x_ref, g_in_ref, g_out_ref, wg_ref, wu_ref, wd_ref, o_ref, hn_ref, acc_ref):
    j = pl.program_id(1)

    @pl.when(j == 0)
    def _():
        hn_ref[...] = _rms_norm(x_ref[...], g_in_ref[...]).astype(BF16)
        acc_ref[...] = jnp.zeros(acc_ref.shape, F32)

    h = hn_ref[...]
    gate = _dot(h, wg_ref[...])
    up = _dot(h, wu_ref[...])
    act = (gate * jax.nn.sigmoid(gate)) * up
    acc_ref[...] += _dot(act.astype(BF16), wd_ref[...])

    @pl.when(j == pl.num_programs(1) - 1)
    def _():
        o_ref[...] = x_ref[...] + _rms_norm(acc_ref[...], g_out_ref[...])


def _ffn(x, g_in, g_out, wg, wu, wd, *, tm=512, tf=512):
    t, d = x.shape
    f = wg.shape[1]
    assert t % tm == 0 and f % tf == 0
    return pl.pallas_call(
        _ffn_kernel,
        grid=(t // tm, f // tf),
        in_specs=[
            pl.BlockSpec((tm, d), lambda i, j: (i, 0)),
            pl.BlockSpec((1, d), lambda i, j: (0, 0)),
            pl.BlockSpec((1, d), lambda i, j: (0, 0)),
            pl.BlockSpec((d, tf), lambda i, j: (0, j)),
            pl.BlockSpec((d, tf), lambda i, j: (0, j)),
            pl.BlockSpec((tf, d), lambda i, j: (j, 0)),
        ],
        out_specs=pl.BlockSpec((tm, d), lambda i, j: (i, 0)),
        out_shape=jax.ShapeDtypeStruct((t, d), F32),
        scratch_shapes=[pltpu.VMEM((tm, d), BF16), pltpu.VMEM((tm, d), F32)],
        compiler_params=_params(("parallel", "arbitrary")),
        name="ffn",
    )(x, g_in, g_out, wg, wu, wd)


def _sb_suffix(z, tri2, mask):
    tb = SB_BLOCK
    soft = jnp.maximum(z, 0.0) + jnp.log(1.0 + jnp.exp2(-jnp.abs(z))) * LOG2E
    if mask is not None:
        soft = jnp.where(mask, soft, 0.0)
    hi = soft.astype(BF16)
    lo = (soft - hi.astype(F32)).astype(BF16)
    left, right = slice(0, tb), slice(tb, 2 * tb)
    suffix_l = _dot(jnp.concatenate([hi[:, left], lo[:, left]], axis=1), tri2)
    suffix_r = _dot(jnp.concatenate([hi[:, right], lo[:, right]], axis=1), tri2)
    total_l = jnp.sum(soft[:, left], axis=-1, keepdims=True)
    total_r = jnp.sum(soft[:, right], axis=-1, keepdims=True)
    return suffix_l, suffix_r, total_l, total_r


def _sb_weights(z, sums, carry, mask):
    tb = SB_BLOCK
    suffix_l, suffix_r, total_l, total_r = sums
    expo = jnp.concatenate([z[:, :tb] - suffix_l - (carry + total_r), z[:, tb:] - suffix_r - carry], axis=1)
    a = jnp.exp2(expo)
    if mask is not None:
        a = jnp.where(mask, a, 0.0)
    return a.astype(BF16), carry + total_r + total_l


def _sb_kernel(q_ref, k_ref, v_ref, tri_ref, o_ref, *, heads):
    qi = pl.program_id(2)
    tq = SB_TILE
    tri2 = tri_ref[...]
    row = lax.broadcasted_iota(jnp.int32, (tq, tq), 0)
    col = lax.broadcasted_iota(jnp.int32, (tq, tq), 1)
    strictly_past = col < row

    def head_cols(h):
        return slice(h * HEAD_DIM, (h + 1) * HEAD_DIM)

    qs = [q_ref[:, head_cols(h)] for h in range(heads)]

    def tiles(rows, accs, carries, mask):
        zs = [_dot_nt(qs[h], k_ref[rows, head_cols(h)]) for h in range(heads)]
        sums = [_sb_suffix(zs[h], tri2, mask) for h in range(heads)]
        out = []
        for h in range(heads):
            a, carry = _sb_weights(zs[h], sums[h], carries[h], mask)
            out += [accs[h] + _dot(a, v_ref[rows, head_cols(h)]), carry]
        return out

    diag = pl.ds(pl.multiple_of(qi * tq, tq), tq)
    state = tiles(diag, [jnp.zeros((tq, HEAD_DIM), F32)] * heads, [jnp.zeros((tq, 1), F32)] * heads, strictly_past)

    def body(it, st):
        rows = pl.ds(pl.multiple_of((qi - 1 - it) * tq, tq), tq)
        return tuple(tiles(rows, st[0::2], st[1::2], None))

    state = lax.fori_loop(0, qi, body, tuple(state))
    for h in range(heads):
        o_ref[:, head_cols(h)] = state[2 * h].astype(o_ref.dtype)


def _stick_breaking(proj, *, batch, seq, col0, heads_per_step=2):
    tb, tq = SB_BLOCK, SB_TILE
    assert tq == 2 * tb and seq % tq == 0
    width = heads_per_step * HEAD_DIM
    c0 = col0 // width
    per = ATT_HEADS // heads_per_step
    nq = seq // tq
    tri = lax.broadcasted_iota(jnp.int32, (tb, tb), 0) >= lax.broadcasted_iota(jnp.int32, (tb, tb), 1)
    tri2 = jnp.concatenate([tri, tri], axis=0).astype(BF16)
    kern = functools.partial(_sb_kernel, heads=heads_per_step)
    return pl.pallas_call(
        kern,
        grid=(batch, per, nq),
        in_specs=[
            pl.BlockSpec((tq, width), lambda b, h, i: (b * nq + i, c0 + h)),
            pl.BlockSpec((seq, width), lambda b, h, i: (b, c0 + per + h)),
            pl.BlockSpec((seq, width), lambda b, h, i: (b, c0 + 2 * per + h)),
            pl.BlockSpec((2 * tb, tb), lambda b, h, i: (0, 0)),
        ],
        out_specs=pl.BlockSpec((tq, width), lambda b, h, i: (b * nq + i, h)),
        out_shape=jax.ShapeDtypeStruct((batch * seq, ATT_HEADS * HEAD_DIM), BF16),
        compiler_params=_params(("parallel", "parallel", "arbitrary")),
        name="stick_breaking",
    )(proj, proj, proj, tri2)


def _moba_setup(k, kaug_ref, kmean_ref, *, seq):
    blk = MOBA_BLOCK
    nb = kmean_ref.shape[1]
    lane = lax.broadcasted_iota(jnp.int32, (blk, HEAD_DIM), 1)
    for n in range(seq // blk):
        kaug_ref[n * blk:(n + 1) * blk, :HEAD_DIM] = k[n * blk:(n + 1) * blk, :]
        kaug_ref[n * blk:(n + 1) * blk, HEAD_DIM:] = jnp.where(lane == n, 1.0, 0.0).astype(BF16)
    key = lax.broadcasted_iota(jnp.int32, (nb, seq), 1)
    first = lax.broadcasted_iota(jnp.int32, (nb, seq), 0) * blk
    in_block = jnp.logical_and(key >= first, key < first + blk)
    mean = _dot(jnp.where(in_block, 1.0 / blk, 0.0).astype(BF16), k)
    hi = mean.astype(BF16)
    mid = (mean - hi.astype(F32)).astype(BF16)
    lo = (mean - hi.astype(F32) - mid.astype(F32)).astype(BF16)
    kmean_ref[0] = hi
    kmean_ref[1] = mid
    kmean_ref[2] = lo


def _moba_select(q, qi, kmean_ref):
    blk = MOBA_BLOCK
    nb = kmean_ref.shape[1]
    gate = _dot_nt(kmean_ref[0], q) + _dot_nt(kmean_ref[1], q) + _dot_nt(kmean_ref[2], q)
    blk_id = lax.broadcasted_iota(jnp.int32, (nb, blk), 0)
    past = blk_id < qi
    gate = jnp.where(past, gate, -jnp.inf)
    beaten = jnp.zeros((nb, blk), F32)
    for m in range(nb):
        gm = gate[m:m + 1, :]
        ahead = jnp.logical_or(gm > gate, jnp.logical_and(gm == gate, blk_id > m))
        beaten = beaten + jnp.where(ahead, 1.0, 0.0)
    keep = jnp.logical_or(jnp.logical_and(past, beaten < MOBA_TOPK), blk_id == qi)
    bias = jnp.where(keep, 0.0, MOBA_UNSELECTED)
    bias = jnp.concatenate([bias, jnp.zeros((HEAD_DIM - nb, blk), F32)], axis=0)
    return jnp.concatenate([q, bias.T.astype(BF16)], axis=1)


def _moba_kernel(q_ref, k_ref, v_ref, o_ref, kaug_ref, kmean_ref, *, seq, heads):
    qi = pl.program_id(2)
    blk = MOBA_BLOCK
    pair = 2 * blk

    def head_cols(h):
        return slice(h * HEAD_DIM, (h + 1) * HEAD_DIM)

    @pl.when(qi == 0)
    def _():
        for h in range(heads):
            _moba_setup(k_ref[:, head_cols(h)], kaug_ref.at[h], kmean_ref.at[h], seq=seq)

    qa = [_moba_select(q_ref[:, head_cols(h)], qi, kmean_ref.at[h]) for h in range(heads)]

    def scores(h, j):
        rows = pl.ds(pl.multiple_of(j * pair, pair), pair)
        return _dot_nt(qa[h], kaug_ref[h, rows, :]), v_ref[rows, head_cols(h)]

    last = lax.shift_right_logical(qi, 1)
    row = lax.broadcasted_iota(jnp.int32, (blk, pair), 0)
    col = lax.broadcasted_iota(jnp.int32, (blk, pair), 1)
    causal = col <= row + (qi - 2 * last) * blk
    state = []
    for h in range(heads):
        s, v = scores(h, last)
        s = jnp.where(causal, s, MASK_VALUE)
        m = jnp.max(s, axis=-1, keepdims=True)
        p = jnp.exp2(s - m)
        state += [m, jnp.sum(p, axis=-1, keepdims=True), _dot(p.astype(BF16), v)]

    def body(j, st):
        out = []
        nxt = scores(0, j)
        for h in range(heads):
            s, v = nxt
            if h + 1 < heads:
                nxt = scores(h + 1, j)
            m_old, l_old, acc = st[3 * h:3 * h + 3]
            m_new = jnp.maximum(m_old, jnp.max(s, axis=-1, keepdims=True))
            alpha = jnp.exp2(m_old - m_new)
            p = jnp.exp2(s - m_new)
            out += [m_new, alpha * l_old + jnp.sum(p, axis=-1, keepdims=True),
                    alpha * acc + _dot(p.astype(BF16), v)]
        return tuple(out)

    state = lax.fori_loop(0, last, body, tuple(state))
    for h in range(heads):
        o_ref[:, head_cols(h)] = (state[3 * h + 2] / state[3 * h + 1]).astype(o_ref.dtype)


def _moba(proj, *, batch, seq, col0, heads_per_step=4):
    blk = MOBA_BLOCK
    nq = seq // blk
    assert nq % 2 == 0
    nb_pad = -(-nq // 16) * 16
    assert nb_pad <= HEAD_DIM
    width = heads_per_step * HEAD_DIM
    c0 = col0 // width
    per = ATT_HEADS // heads_per_step
    kern = functools.partial(_moba_kernel, seq=seq, heads=heads_per_step)
    return pl.pallas_call(
        kern,
        grid=(batch, per, nq),
        in_specs=[
            pl.BlockSpec((blk, width), lambda b, h, i: (b * nq + i, c0 + h)),
            pl.BlockSpec((seq, width), lambda b, h, i: (b, c0 + per + h)),
            pl.BlockSpec((seq, width), lambda b, h, i: (b, c0 + 2 * per + h)),
        ],
        out_specs=pl.BlockSpec((blk, width), lambda b, h, i: (b * nq + i, h)),
        out_shape=jax.ShapeDtypeStruct((batch * seq, ATT_HEADS * HEAD_DIM), BF16),
        scratch_shapes=[
            pltpu.VMEM((heads_per_step, seq, 2 * HEAD_DIM), BF16),
            pltpu.VMEM((heads_per_step, 3, nb_pad, HEAD_DIM), BF16),
        ],
        compiler_params=_params(("parallel", "parallel", "arbitrary")),
        name="moba",
    )(proj, proj, proj)


def _rope_tables(seq):
    inv = 1.0 / (ROPE_THETA ** (jnp.arange(0, HEAD_DIM, 2, dtype=F32) / HEAD_DIM))
    ang = jnp.arange(seq, dtype=F32)[:, None] * inv[None, :]
    cos, sin = jnp.cos(ang), jnp.sin(ang)
    return jnp.concatenate([cos, cos], axis=-1), jnp.concatenate([-sin, sin], axis=-1)


def _colscale(n, q_ranges):
    cs = jnp.ones((1, n), F32)
    for lo, hi in q_ranges:
        cs = cs.at[:, lo:hi].set(Q_SCALE)
    return cs


def kernel(x, norm_gains, w_in_ab, pool_w, pool_scale, w_out_ab, w_in_cd, w_out_cd, ffn_gate, ffn_up, ffn_down):
    batch, seq, d = x.shape
    att = ATT_HEADS * HEAD_DIM
    pool_width = len(POOL_WINDOWS) * POOL_GROUP
    n_groups = len(DIL_PATTERNS)
    cos, sin = _rope_tables(seq)
    gains = norm_gains.reshape(norm_gains.shape[0], 4, 1, d)
    xs = x.reshape(batch * seq, d)

    qkv0 = pool_width
    proj = _inproj(xs, gains[0, 0], w_in_ab[0].astype(BF16),
                   _colscale(w_in_ab.shape[2], [(qkv0, qkv0 + n_groups * att)]), cos, sin, seq=seq,
                   rope_cols=(qkv0, qkv0 + 2 * n_groups * att))
    a_out = _pool(proj, pool_w[0].astype(BF16), pool_scale[0].reshape(1, pool_width), seq=seq)
    b_out = _dilated(proj, batch=batch, seq=seq, col0=qkv0)
    xs = _outproj(a_out, b_out, w_out_ab[0].astype(BF16), xs, gains[0, 1])
    xs = _ffn(xs, gains[0, 2], gains[0, 3], ffn_gate[0].astype(BF16), ffn_up[0].astype(BF16),
              ffn_down[0].astype(BF16))

    moba0 = 3 * att
    proj = _inproj(xs, gains[1, 0], w_in_cd[0].astype(BF16),
                   _colscale(w_in_cd.shape[2], [(0, att), (moba0, moba0 + att)]), cos, sin, seq=seq,
                   rope_cols=(moba0, moba0 + 2 * att))
    c_out = _stick_breaking(proj, batch=batch, seq=seq, col0=0)
    d_out = _moba(proj, batch=batch, seq=seq, col0=moba0)
    xs = _outproj(c_out, d_out, w_out_cd[0].astype(BF16), xs, gains[1, 1])
    xs = _ffn(xs, gains[1, 2], gains[1, 3], ffn_gate[1].astype(BF16), ffn_up[1].astype(BF16),
              ffn_down[1].astype(BF16))
    return xs.reshape(batch, seq, d)
```

```python
import functools

import jax
import jax.numpy as jnp
from jax import lax
from jax.experimental import pallas as pl
from jax.experimental.pallas import tpu as pltpu

F32 = jnp.float32
BF16 = jnp.bfloat16

HEAD_DIM = 128
EPS = 1e-6
ROPE_THETA = 10000.0
POOL_WINDOWS = (2, 4, 8, 16)
POOL_GROUP = 256
POOL_HALO = 16
DIL_PATTERNS = ((128, 1), (512, 4), (2048, 16))
DIL_QBLOCK = 128
DIL_UNROLL = 8
ATT_HEADS = 8
MOBA_BLOCK = 256
MOBA_TOPK = 3
SB_BLOCK = 256
SB_TILE = 2 * SB_BLOCK
SB_LINEAR_ABOVE = 64.0
LOG2E = 1.4426950408889634
Q_SCALE = LOG2E * HEAD_DIM ** -0.5
MASK_VALUE = -1e30
MOBA_UNSELECTED = -30000.0

V7X_VMEM_BYTES = 64 * 1024 * 1024
VMEM_LIMIT = V7X_VMEM_BYTES - 8 * 1024 * 1024


def _params(semantics):
    return pltpu.CompilerParams(dimension_semantics=semantics, vmem_limit_bytes=VMEM_LIMIT)


def _dot(a, b):
    return jnp.dot(a, b, preferred_element_type=F32)


def _dot_nt(a, b):
    return lax.dot_general(a, b, (((1,), (1,)), ((), ())), preferred_element_type=F32)


def _rms_norm(x, gain):
    return x * lax.rsqrt(jnp.mean(x * x, axis=-1, keepdims=True) + EPS) * gain


def _inproj_kernel(x_ref, g_ref, w_ref, cs_ref, cos_ref, sin_ref, o_ref, hn_ref, *, rope_lo, rope_hi):
    j = pl.program_id(1)

    @pl.when(j == 0)
    def _():
        hn_ref[...] = _rms_norm(x_ref[...], g_ref[...]).astype(BF16)

    acc = _dot(hn_ref[...], w_ref[...]) * cs_ref[...]
    is_rope = jnp.logical_and(j >= rope_lo, j < rope_hi)
    cos = jnp.where(is_rope, cos_ref[...], 1.0)
    sin = jnp.where(is_rope, sin_ref[...], 0.0)
    for c in range(acc.shape[1] // HEAD_DIM):
        blk = acc[:, c * HEAD_DIM:(c + 1) * HEAD_DIM]
        rot = pltpu.roll(blk, HEAD_DIM // 2, 1)
        o_ref[:, c * HEAD_DIM:(c + 1) * HEAD_DIM] = (blk * cos + rot * sin).astype(o_ref.dtype)


def _inproj(x, gain, w, colscale, cos, sin, *, seq, rope_cols, tm=1024, tn=1024):
    t, d = x.shape
    n = w.shape[1]
    assert t % tm == 0 and n % tn == 0 and seq % tm == 0
    assert rope_cols[0] % tn == 0 and rope_cols[1] % tn == 0
    pos_blocks = seq // tm
    kern = functools.partial(_inproj_kernel, rope_lo=rope_cols[0] // tn, rope_hi=rope_cols[1] // tn)
    return pl.pallas_call(
        kern,
        grid=(t // tm, n // tn),
        in_specs=[
            pl.BlockSpec((tm, d), lambda i, j: (i, 0)),
            pl.BlockSpec((1, d), lambda i, j: (0, 0)),
            pl.BlockSpec((d, tn), lambda i, j: (0, j)),
            pl.BlockSpec((1, tn), lambda i, j: (0, j)),
            pl.BlockSpec((tm, HEAD_DIM), lambda i, j: (i % pos_blocks, 0)),
            pl.BlockSpec((tm, HEAD_DIM), lambda i, j: (i % pos_blocks, 0)),
        ],
        out_specs=pl.BlockSpec((tm, tn), lambda i, j: (i, j)),
        out_shape=jax.ShapeDtypeStruct((t, n), BF16),
        scratch_shapes=[pltpu.VMEM((tm, d), BF16)],
        compiler_params=_params(("parallel", "arbitrary")),
        name="inproj",
    )(x, gain, w, colscale, cos, sin)


def _pool_kernel(u_ref, halo_ref, w_ref, sc_ref, o_ref, ext_ref, *, tm, seq):
    pos0 = (pl.program_id(0) * tm) % seq
    ext_ref[pl.ds(POOL_HALO, tm), :] = u_ref[...].astype(F32)
    ext_ref[pl.ds(0, POOL_HALO), :] = jnp.where(pos0 == 0, 0.0, halo_ref[...].astype(F32))
    pos = pos0 + lax.broadcasted_iota(jnp.int32, (tm, 1), 0)
    for g, w in enumerate(POOL_WINDOWS):
        cols = pl.ds(g * POOL_GROUP, POOL_GROUP)
        cur = ext_ref[pl.ds(POOL_HALO, tm), cols]
        tot = cur
        for back in range(1, w):
            tot = tot + ext_ref[pl.ds(POOL_HALO - back, tm), cols]
        cnt = jnp.minimum(pos + 1, w).astype(F32)
        pooled = tot / cnt - cur
        y = _dot(pooled.astype(BF16), w_ref[g])
        o_ref[:, g * POOL_GROUP:(g + 1) * POOL_GROUP] = (
            y * sc_ref[:, g * POOL_GROUP:(g + 1) * POOL_GROUP]).astype(o_ref.dtype)


def _pool(proj, w, scale, *, seq, tm=512):
    t = proj.shape[0]
    width = len(POOL_WINDOWS) * POOL_GROUP
    assert seq % tm == 0 and tm % POOL_HALO == 0
    halo_blocks = tm // POOL_HALO
    kern = functools.partial(_pool_kernel, tm=tm, seq=seq)
    return pl.pallas_call(
        kern,
        grid=(t // tm,),
        in_specs=[
            pl.BlockSpec((tm, width), lambda i: (i, 0)),
            pl.BlockSpec((POOL_HALO, width), lambda i: (jnp.maximum(i * halo_blocks - 1, 0), 0)),
            pl.BlockSpec(w.shape, lambda i: (0, 0, 0)),
            pl.BlockSpec((1, width), lambda i: (0, 0)),
        ],
        out_specs=pl.BlockSpec((tm, width), lambda i: (i, 0)),
        out_shape=jax.ShapeDtypeStruct((t, width), BF16),
        scratch_shapes=[pltpu.VMEM((tm + POOL_HALO, width), F32)],
        compiler_params=_params(("parallel",)),
        name="pool",
    )(proj, proj, w, scale)


def _dil_kernel(q_ref, k_ref, v_ref, o_ref, qd_ref, kd_ref, vd_ref, qw_ref, kw_ref, vw_ref,
                m_ref, l_ref, acc_ref, *, seq):
    g = pl.program_id(2)
    wide_refs = (qw_ref, kw_ref, vw_ref)
    qb = DIL_QBLOCK
    nblocks = seq // qb

    @pl.when(g == 0)
    def _():
        m_ref[...] = jnp.full(m_ref.shape, MASK_VALUE, F32)
        l_ref[...] = jnp.zeros(l_ref.shape, F32)
        acc_ref[...] = jnp.zeros(acc_ref.shape, F32)
        kd_ref[pl.ds(0, qb), :] = jnp.zeros((qb, HEAD_DIM), BF16)
        vd_ref[pl.ds(0, qb), :] = jnp.zeros((qb, HEAD_DIM), BF16)

    row = lax.broadcasted_iota(jnp.int32, (qb, 2 * qb), 0)
    col = lax.broadcasted_iota(jnp.int32, (qb, 2 * qb), 1)
    band = jnp.logical_and(col >= row, col <= row + qb)
    own_block = col >= qb

    def run_group(dil):
        nblk = seq // dil // qb
        assert nblk & (nblk - 1) == 0
        shift = nblk.bit_length() - 1

        def positions(idx):
            r = lax.shift_right_logical(idx, shift)
            n = jnp.bitwise_and(idx, nblk - 1)
            start = r + dil * qb * n
            return (pl.ds(start, qb, stride=dil) if dil > 1 else pl.ds(pl.multiple_of(start, qb), qb)), n

        if dil > 1:
            def widen(idx, carry):
                rows = pl.ds(pl.multiple_of(idx * qb, qb), qb)
                for src_ref, wide_ref in zip((q_ref, k_ref, v_ref), wide_refs):
                    wide_ref[rows, :] = src_ref[rows, :].astype(F32)
                return carry

            lax.fori_loop(0, nblocks, widen, 0, unroll=4)
            sources = wide_refs
        else:
            sources = (q_ref, k_ref, v_ref)

        def gather(idx, carry):
            src, _ = positions(idx)
            qd_ref[pl.ds(pl.multiple_of(idx * qb, qb), qb), :] = sources[0][src, :].astype(BF16)
            dst = pl.ds(pl.multiple_of((idx + 1) * qb, qb), qb)
            kd_ref[dst, :] = sources[1][src, :].astype(BF16)
            vd_ref[dst, :] = sources[2][src, :].astype(BF16)
            return carry

        lax.fori_loop(0, nblocks, gather, 0, unroll=4)

        def window(idx):
            return pl.ds(pl.multiple_of(idx * qb, qb), 2 * qb)

        def block_scores(idx):
            q = qd_ref[pl.ds(pl.multiple_of(idx * qb, qb), qb), :]
            return _dot_nt(q, kd_ref[window(idx), :])

        def block_update(idx, s):
            cur, n = positions(idx)
            s = jnp.where(jnp.logical_and(band, jnp.logical_or(own_block, n > 0)), s, MASK_VALUE)
            m_old = m_ref[cur, :]
            m_new = jnp.maximum(m_old, jnp.max(s, axis=-1, keepdims=True))
            p = jnp.exp2(s - jnp.concatenate([m_new, m_new], axis=1))
            alpha = jnp.exp2(m_old - m_new)
            l_ref[cur, :] = alpha * l_ref[cur, :] + jnp.sum(p, axis=-1, keepdims=True)
            acc_ref[cur, :] = alpha * acc_ref[cur, :] + _dot(p.astype(BF16), vd_ref[window(idx), :])
            m_ref[cur, :] = m_new

        def body(it, carry):
            first = it * DIL_UNROLL
            scores = [block_scores(first + u) for u in range(DIL_UNROLL)]
            for u in range(DIL_UNROLL):
                block_update(first + u, scores[u])
            return carry

        lax.fori_loop(0, nblocks // DIL_UNROLL, body, 0)

    for gi, (_, dil) in enumerate(DIL_PATTERNS):
        pl.when(g == gi)(functools.partial(run_group, dil))

    @pl.when(g == len(DIL_PATTERNS) - 1)
    def _():
        o_ref[...] = (acc_ref[...] / l_ref[...]).astype(o_ref.dtype)


def _dilated(proj, *, batch, seq, col0):
    n_groups = len(DIL_PATTERNS)
    per = n_groups * ATT_HEADS
    c0 = col0 // HEAD_DIM
    kern = functools.partial(_dil_kernel, seq=seq)

    def spec(which):
        return pl.BlockSpec((seq, HEAD_DIM), lambda b, h, g: (b, c0 + which * per + g * ATT_HEADS + h))

    return pl.pallas_call(
        kern,
        grid=(batch, ATT_HEADS, n_groups),
        in_specs=[spec(0), spec(1), spec(2)],
        out_specs=pl.BlockSpec((seq, HEAD_DIM), lambda b, h, g: (b, h)),
        out_shape=jax.ShapeDtypeStruct((batch * seq, ATT_HEADS * HEAD_DIM), BF16),
        scratch_shapes=[
            pltpu.VMEM((seq, HEAD_DIM), BF16),
            pltpu.VMEM((seq + DIL_QBLOCK, HEAD_DIM), BF16),
            pltpu.VMEM((seq + DIL_QBLOCK, HEAD_DIM), BF16),
        ] + [pltpu.VMEM((seq, HEAD_DIM), F32)] * 6,
        compiler_params=_params(("parallel", "parallel", "arbitrary")),
        name="dilated",
    )(proj, proj, proj)


def _outproj_kernel(a_ref, b_ref, wa_ref, wb_ref, x_ref, g_ref, o_ref):
    y = _dot(a_ref[...], wa_ref[...]) + _dot(b_ref[...], wb_ref[...])
    o_ref[...] = x_ref[...] + _rms_norm(y, g_ref[...])


def _outproj(a, b, w, x, gain, *, tm=256):
    t, d = x.shape
    half = a.shape[1]
    assert w.shape == (2 * half, d) and t % tm == 0
    return pl.pallas_call(
        _outproj_kernel,
        grid=(t // tm,),
        in_specs=[
            pl.BlockSpec((tm, half), lambda i: (i, 0)),
            pl.BlockSpec((tm, half), lambda i: (i, 0)),
            pl.BlockSpec((half, d), lambda i: (0, 0)),
            pl.BlockSpec((half, d), lambda i: (1, 0)),
            pl.BlockSpec((tm, d), lambda i: (i, 0)),
            pl.BlockSpec((1, d), lambda i: (0, 0)),
        ],
        out_specs=pl.BlockSpec((tm, d), lambda i: (i, 0)),
        out_shape=jax.ShapeDtypeStruct((t, d), F32),
        compiler_params=_params(("parallel",)),
        name="outproj",
    )(a, b, w, w, x, gain)


def _ffn_kernel(x_ref, g_in_ref, g_out_ref, wg_ref, wu_ref, wd_ref, o_ref, hn_ref, acc_ref):
    j = pl.program_id(1)

    @pl.when(j == 0)
    def _():
        hn_ref[...] = _rms_norm(x_ref[...], g_in_ref[...]).astype(BF16)
        acc_ref[...] = jnp.zeros(acc_ref.shape, F32)

    h = hn_ref[...]
    gate = _dot(h, wg_ref[...])
    up = _dot(h, wu_ref[...])
    act = (gate * jax.nn.sigmoid(gate)) * up
    acc_ref[...] += _dot(act.astype(BF16), wd_ref[...])

    @pl.when(j == pl.num_programs(1) - 1)
    def _():
        o_ref[...] = x_ref[...] + _rms_norm(acc_ref[...], g_out_ref[...])


def _ffn(x, g_in, g_out, wg, wu, wd, layer, *, tm=512, tf=512):
    t, d = x.shape
    f = wg.shape[2]
    assert t % tm == 0 and f % tf == 0
    return pl.pallas_call(
        _ffn_kernel,
        grid=(t // tm, f // tf),
        in_specs=[
            pl.BlockSpec((tm, d), lambda i, j: (i, 0)),
            pl.BlockSpec((1, d), lambda i, j: (0, 0)),
            pl.BlockSpec((1, d), lambda i, j: (0, 0)),
            pl.BlockSpec((None, d, tf), lambda i, j: (layer, 0, j)),
            pl.BlockSpec((None, d, tf), lambda i, j: (layer, 0, j)),
            pl.BlockSpec((None, tf, d), lambda i, j: (layer, j, 0)),
        ],
        out_specs=pl.BlockSpec((tm, d), lambda i, j: (i, 0)),
        out_shape=jax.ShapeDtypeStruct((t, d), F32),
        scratch_shapes=[pltpu.VMEM((tm, d), BF16), pltpu.VMEM((tm, d), F32)],
        compiler_params=_params(("parallel", "arbitrary")),
        name="ffn",
    )(x, g_in, g_out, wg, wu, wd)


def _sb_suffix(z, tri, mask):
    tb = SB_BLOCK
    soft = jnp.where(z > SB_LINEAR_ABOVE, z, jnp.log(1.0 + jnp.exp2(z)) * LOG2E)
    if mask is not None:
        soft = jnp.where(mask, soft, 0.0)
    p16 = soft.astype(BF16)
    left, right = slice(0, tb), slice(tb, 2 * tb)
    suffix_l = _dot(p16[:, left], tri)
    suffix_r = _dot(p16[:, right], tri)
    total_l = jnp.sum(soft[:, left], axis=-1, keepdims=True)
    total_r = jnp.sum(soft[:, right], axis=-1, keepdims=True)
    return suffix_l, suffix_r, total_l, total_r


def _sb_weights(z, sums, carry, mask):
    tb = SB_BLOCK
    suffix_l, suffix_r, total_l, total_r = sums
    expo = jnp.concatenate([z[:, :tb] - suffix_l - (carry + total_r), z[:, tb:] - suffix_r - carry], axis=1)
    a = jnp.exp2(expo)
    if mask is not None:
        a = jnp.where(mask, a, 0.0)
    return a.astype(BF16), carry + total_r + total_l


def _sb_kernel(q_ref, k_ref, v_ref, tri_ref, o_ref, *, heads):
    qi = pl.program_id(2)
    tq = SB_TILE
    tri = tri_ref[...]
    row = lax.broadcasted_iota(jnp.int32, (tq, tq), 0)
    col = lax.broadcasted_iota(jnp.int32, (tq, tq), 1)
    strictly_past = col < row

    def head_cols(h):
        return slice(h * HEAD_DIM, (h + 1) * HEAD_DIM)

    qs = [q_ref[:, head_cols(h)] for h in range(heads)]

    def tiles(rows, accs, carries, mask):
        zs = [_dot_nt(qs[h], k_ref[rows, head_cols(h)]) for h in range(heads)]
        sums = [_sb_suffix(zs[h], tri, mask) for h in range(heads)]
        out = []
        for h in range(heads):
            a, carry = _sb_weights(zs[h], sums[h], carries[h], mask)
            out += [accs[h] + _dot(a, v_ref[rows, head_cols(h)]), carry]
        return out

    diag = pl.ds(pl.multiple_of(qi * tq, tq), tq)
    state = tiles(diag, [jnp.zeros((tq, HEAD_DIM), F32)] * heads, [jnp.zeros((tq, 1), F32)] * heads, strictly_past)

    def body(it, st):
        rows = pl.ds(pl.multiple_of((qi - 1 - it) * tq, tq), tq)
        return tuple(tiles(rows, st[0::2], st[1::2], None))

    state = lax.fori_loop(0, qi, body, tuple(state))
    for h in range(heads):
        o_ref[:, head_cols(h)] = state[2 * h].astype(o_ref.dtype)


def _stick_breaking(proj, *, batch, seq, col0, heads_per_step=2):
    tb, tq = SB_BLOCK, SB_TILE
    assert tq == 2 * tb and seq % tq == 0
    width = heads_per_step * HEAD_DIM
    c0 = col0 // width
    per = ATT_HEADS // heads_per_step
    nq = seq // tq
    tri = (lax.broadcasted_iota(jnp.int32, (tb, tb), 0) >= lax.broadcasted_iota(jnp.int32, (tb, tb), 1)).astype(BF16)
    kern = functools.partial(_sb_kernel, heads=heads_per_step)
    return pl.pallas_call(
        kern,
        grid=(batch, per, nq),
        in_specs=[
            pl.BlockSpec((tq, width), lambda b, h, i: (b * nq + i, c0 + h)),
            pl.BlockSpec((seq, width), lambda b, h, i: (b, c0 + per + h)),
            pl.BlockSpec((seq, width), lambda b, h, i: (b, c0 + 2 * per + h)),
            pl.BlockSpec((tb, tb), lambda b, h, i: (0, 0)),
        ],
        out_specs=pl.BlockSpec((tq, width), lambda b, h, i: (b * nq + i, h)),
        out_shape=jax.ShapeDtypeStruct((batch * seq, ATT_HEADS * HEAD_DIM), BF16),
        compiler_params=_params(("parallel", "parallel", "arbitrary")),
        name="stick_breaking",
    )(proj, proj, proj, tri)


def _moba_setup(k, kaug_ref, kmean_ref, *, seq):
    blk = MOBA_BLOCK
    nb = kmean_ref.shape[1]
    lane = lax.broadcasted_iota(jnp.int32, (blk, HEAD_DIM), 1)
    for n in range(seq // blk):
        kaug_ref[n * blk:(n + 1) * blk, :HEAD_DIM] = k[n * blk:(n + 1) * blk, :]
        kaug_ref[n * blk:(n + 1) * blk, HEAD_DIM:] = jnp.where(lane == n, 1.0, 0.0).astype(BF16)
    key = lax.broadcasted_iota(jnp.int32, (nb, seq), 1)
    first = lax.broadcasted_iota(jnp.int32, (nb, seq), 0) * blk
    in_block = jnp.logical_and(key >= first, key < first + blk)
    mean = _dot(jnp.where(in_block, 1.0 / blk, 0.0).astype(BF16), k)
    hi = mean.astype(BF16)
    mid = (mean - hi.astype(F32)).astype(BF16)
    lo = (mean - hi.astype(F32) - mid.astype(F32)).astype(BF16)
    kmean_ref[0] = hi
    kmean_ref[1] = mid
    kmean_ref[2] = lo


def _moba_select(q, qi, kmean_ref):
    blk = MOBA_BLOCK
    nb = kmean_ref.shape[1]
    gate = _dot_nt(kmean_ref[0], q) + _dot_nt(kmean_ref[1], q) + _dot_nt(kmean_ref[2], q)
    blk_id = lax.broadcasted_iota(jnp.int32, (nb, blk), 0)
    past = blk_id < qi
    gate = jnp.where(past, gate, -jnp.inf)
    beaten = jnp.zeros((nb, blk), F32)
    for m in range(nb):
        gm = gate[m:m + 1, :]
        ahead = jnp.logical_or(gm > gate, jnp.logical_and(gm == gate, blk_id > m))
        beaten = beaten + jnp.where(ahead, 1.0, 0.0)
    keep = jnp.logical_or(jnp.logical_and(past, beaten < MOBA_TOPK), blk_id == qi)
    bias = jnp.where(keep, 0.0, MOBA_UNSELECTED)
    bias = jnp.concatenate([bias, jnp.zeros((HEAD_DIM - nb, blk), F32)], axis=0)
    return jnp.concatenate([q, bias.T.astype(BF16)], axis=1)


def _moba_kernel(q_ref, k_ref, v_ref, o_ref, kaug_ref, kmean_ref, *, seq, heads):
    qi = pl.program_id(2)
    blk = MOBA_BLOCK
    pair = 2 * blk

    def head_cols(h):
        return slice(h * HEAD_DIM, (h + 1) * HEAD_DIM)

    @pl.when(qi == 0)
    def _():
        for h in range(heads):
            _moba_setup(k_ref[:, head_cols(h)], kaug_ref.at[h], kmean_ref.at[h], seq=seq)

    qa = [_moba_select(q_ref[:, head_cols(h)], qi, kmean_ref.at[h]) for h in range(heads)]

    def scores(h, j):
        rows = pl.ds(pl.multiple_of(j * pair, pair), pair)
        return _dot_nt(qa[h], kaug_ref[h, rows, :]), v_ref[rows, head_cols(h)]

    last = lax.shift_right_logical(qi, 1)
    row = lax.broadcasted_iota(jnp.int32, (blk, pair), 0)
    col = lax.broadcasted_iota(jnp.int32, (blk, pair), 1)
    causal = col <= row + (qi - 2 * last) * blk
    state = []
    for h in range(heads):
        s, v = scores(h, last)
        s = jnp.where(causal, s, MASK_VALUE)
        m = jnp.max(s, axis=-1, keepdims=True)
        p = jnp.exp2(s - m)
        state += [m, jnp.sum(p, axis=-1, keepdims=True), _dot(p.astype(BF16), v)]

    def body(j, st):
        out = []
        nxt = scores(0, j)
        for h in range(heads):
            s, v = nxt
            if h + 1 < heads:
                nxt = scores(h + 1, j)
            m_old, l_old, acc = st[3 * h:3 * h + 3]
            m_new = jnp.maximum(m_old, jnp.max(s, axis=-1, keepdims=True))
            alpha = jnp.exp2(m_old - m_new)
            p = jnp.exp2(s - m_new)
            out += [m_new, alpha * l_old + jnp.sum(p, axis=-1, keepdims=True),
                    alpha * acc + _dot(p.astype(BF16), v)]
        return tuple(out)

    state = lax.fori_loop(0, last, body, tuple(state))
    for h in range(heads):
        o_ref[:, head_cols(h)] = (state[3 * h + 2] / state[3 * h + 1]).astype(o_ref.dtype)


def _moba(proj, *, batch, seq, col0, heads_per_step=4):
    blk = MOBA_BLOCK
    nq = seq // blk
    assert nq % 2 == 0
    nb_pad = -(-nq // 16) * 16
    assert nb_pad <= HEAD_DIM
    width = heads_per_step * HEAD_DIM
    c0 = col0 // width
    per = ATT_HEADS // heads_per_step
    kern = functools.partial(_moba_kernel, seq=seq, heads=heads_per_step)
    return pl.pallas_call(
        kern,
        grid=(batch, per, nq),
        in_specs=[
            pl.BlockSpec((blk, width), lambda b, h, i: (b * nq + i, c0 + h)),
            pl.BlockSpec((seq, width), lambda b, h, i: (b, c0 + per + h)),
            pl.BlockSpec((seq, width), lambda b, h, i: (b, c0 + 2 * per + h)),
        ],
        out_specs=pl.BlockSpec((blk, width), lambda b, h, i: (b * nq + i, h)),
        out_shape=jax.ShapeDtypeStruct((batch * seq, ATT_HEADS * HEAD_DIM), BF16),
        scratch_shapes=[
            pltpu.VMEM((heads_per_step, seq, 2 * HEAD_DIM), BF16),
            pltpu.VMEM((heads_per_step, 3, nb_pad, HEAD_DIM), BF16),
        ],
        compiler_params=_params(("parallel", "parallel", "arbitrary")),
        name="moba",
    )(proj, proj, proj)


def _rope_tables(seq):
    inv = 1.0 / (ROPE_THETA ** (jnp.arange(0, HEAD_DIM, 2, dtype=F32) / HEAD_DIM))
    ang = jnp.arange(seq, dtype=F32)[:, None] * inv[None, :]
    cos, sin = jnp.cos(ang), jnp.sin(ang)
    return jnp.concatenate([cos, cos], axis=-1), jnp.concatenate([-sin, sin], axis=-1)


def _colscale(n, q_ranges):
    cs = jnp.ones((1, n), F32)
    for lo, hi in q_ranges:
        cs = cs.at[:, lo:hi].set(Q_SCALE)
    return cs


def kernel(x, norm_gains, w_in_ab, pool_w, pool_scale, w_out_ab, w_in_cd, w_out_cd, ffn_gate, ffn_up, ffn_down):
    batch, seq, d = x.shape
    att = ATT_HEADS * HEAD_DIM
    pool_width = len(POOL_WINDOWS) * POOL_GROUP
    n_groups = len(DIL_PATTERNS)
    cos, sin = _rope_tables(seq)
    gains = norm_gains.reshape(norm_gains.shape[0], 4, 1, d)
    xs = x.reshape(batch * seq, d)

    qkv0 = pool_width
    proj = _inproj(xs, gains[0, 0], w_in_ab[0].astype(BF16),
                   _colscale(w_in_ab.shape[2], [(qkv0, qkv0 + n_groups * att)]), cos, sin, seq=seq,
                   rope_cols=(qkv0, qkv0 + 2 * n_groups * att))
    a_out = _pool(proj, pool_w[0].astype(BF16), pool_scale[0].reshape(1, pool_width), seq=seq)
    b_out = _dilated(proj, batch=batch, seq=seq, col0=qkv0)
    xs = _outproj(a_out, b_out, w_out_ab[0].astype(BF16), xs, gains[0, 1])
    ffn_w = (ffn_gate.astype(BF16), ffn_up.astype(BF16), ffn_down.astype(BF16))
    xs = _ffn(xs, gains[0, 2], gains[0, 3], *ffn_w, 0)

    moba0 = 3 * att
    proj = _inproj(xs, gains[1, 0], w_in_cd[0].astype(BF16),
                   _colscale(w_in_cd.shape[2], [(0, att), (moba0, moba0 + att)]), cos, sin, seq=seq,
                   rope_cols=(moba0, moba0 + 2 * att))
    c_out = _stick_breaking(proj, batch=batch, seq=seq, col0=0)
    d_out = _moba(proj, batch=batch, seq=seq, col0=moba0)
    xs = _outproj(c_out, d_out, w_out_cd[0].astype(BF16), xs, gains[1, 1])
    xs = _ffn(xs, gains[1, 2], gains[1, 3], *ffn_w, 1)
    return xs.reshape(batch, seq, d)
```

```python
import functools

import jax
import jax.numpy as jnp
from jax import lax
from jax.experimental import pallas as pl
from jax.experimental.pallas import tpu as pltpu

F32 = jnp.float32
BF16 = jnp.bfloat16

HEAD_DIM = 128
EPS = 1e-6
ROPE_THETA = 10000.0
POOL_WINDOWS = (2, 4, 8, 16)
POOL_GROUP = 256
POOL_HALO = 16
DIL_PATTERNS = ((128, 1), (512, 4), (2048, 16))
DIL_QBLOCK = 128
DIL_UNROLL = 8
ATT_HEADS = 8
MOBA_BLOCK = 256
MOBA_TOPK = 3
SB_BLOCK = 256
SB_TILE = 2 * SB_BLOCK
SB_LINEAR_ABOVE = 64.0
LOG2E = 1.4426950408889634
Q_SCALE = LOG2E * HEAD_DIM ** -0.5
MASK_VALUE = -1e30
MOBA_UNSELECTED = -30000.0

V7X_VMEM_BYTES = 64 * 1024 * 1024
VMEM_LIMIT = V7X_VMEM_BYTES - 8 * 1024 * 1024


def _params(semantics):
    return pltpu.CompilerParams(dimension_semantics=semantics, vmem_limit_bytes=VMEM_LIMIT)


def _dot(a, b):
    return jnp.dot(a, b, preferred_element_type=F32)


def _dot_nt(a, b):
    return lax.dot_general(a, b, (((1,), (1,)), ((), ())), preferred_element_type=F32)


def _rms_norm(x, gain):
    return x * lax.rsqrt(jnp.mean(x * x, axis=-1, keepdims=True) + EPS) * gain


def _inproj_kernel(x_ref, g_ref, w_ref, cs_ref, cos_ref, sin_ref, o_ref, hn_ref, *, rope_lo, rope_hi):
    j = pl.program_id(1)

    @pl.when(j == 0)
    def _():
        hn_ref[...] = _rms_norm(x_ref[...], g_ref[...]).astype(BF16)

    acc = _dot(hn_ref[...], w_ref[...]) * cs_ref[...]
    is_rope = jnp.logical_and(j >= rope_lo, j < rope_hi)
    cos = jnp.where(is_rope, cos_ref[...], 1.0)
    sin = jnp.where(is_rope, sin_ref[...], 0.0)
    for c in range(acc.shape[1] // HEAD_DIM):
        blk = acc[:, c * HEAD_DIM:(c + 1) * HEAD_DIM]
        rot = pltpu.roll(blk, HEAD_DIM // 2, 1)
        o_ref[:, c * HEAD_DIM:(c + 1) * HEAD_DIM] = (blk * cos + rot * sin).astype(o_ref.dtype)


def _inproj(x, gain, w, colscale, cos, sin, *, seq, rope_cols, tm=1024, tn=1024):
    t, d = x.shape
    n = w.shape[1]
    assert t % tm == 0 and n % tn == 0 and seq % tm == 0
    assert rope_cols[0] % tn == 0 and rope_cols[1] % tn == 0
    pos_blocks = seq // tm
    kern = functools.partial(_inproj_kernel, rope_lo=rope_cols[0] // tn, rope_hi=rope_cols[1] // tn)
    return pl.pallas_call(
        kern,
        grid=(t // tm, n // tn),
        in_specs=[
            pl.BlockSpec((tm, d), lambda i, j: (i, 0)),
            pl.BlockSpec((1, d), lambda i, j: (0, 0)),
            pl.BlockSpec((d, tn), lambda i, j: (0, j)),
            pl.BlockSpec((1, tn), lambda i, j: (0, j)),
            pl.BlockSpec((tm, HEAD_DIM), lambda i, j: (i % pos_blocks, 0)),
            pl.BlockSpec((tm, HEAD_DIM), lambda i, j: (i % pos_blocks, 0)),
        ],
        out_specs=pl.BlockSpec((tm, tn), lambda i, j: (i, j)),
        out_shape=jax.ShapeDtypeStruct((t, n), BF16),
        scratch_shapes=[pltpu.VMEM((tm, d), BF16)],
        compiler_params=_params(("parallel", "arbitrary")),
        name="inproj",
    )(x, gain, w, colscale, cos, sin)


def _pool_kernel(u_ref, halo_ref, w_ref, sc_ref, o_ref, ext_ref, *, tm, seq):
    pos0 = (pl.program_id(0) * tm) % seq
    ext_ref[pl.ds(POOL_HALO, tm), :] = u_ref[...].astype(F32)
    ext_ref[pl.ds(0, POOL_HALO), :] = jnp.where(pos0 == 0, 0.0, halo_ref[...].astype(F32))
    pos = pos0 + lax.broadcasted_iota(jnp.int32, (tm, 1), 0)
    for g, w in enumerate(POOL_WINDOWS):
        cols = pl.ds(g * POOL_GROUP, POOL_GROUP)
        cur = ext_ref[pl.ds(POOL_HALO, tm), cols]
        tot = cur
        for back in range(1, w):
            tot = tot + ext_ref[pl.ds(POOL_HALO - back, tm), cols]
        cnt = jnp.minimum(pos + 1, w).astype(F32)
        pooled = tot / cnt - cur
        y = _dot(pooled.astype(BF16), w_ref[g])
        o_ref[:, g * POOL_GROUP:(g + 1) * POOL_GROUP] = (
            y * sc_ref[:, g * POOL_GROUP:(g + 1) * POOL_GROUP]).astype(o_ref.dtype)


def _pool(proj, w, scale, *, seq, tm=512):
    t = proj.shape[0]
    width = len(POOL_WINDOWS) * POOL_GROUP
    assert seq % tm == 0 and tm % POOL_HALO == 0
    halo_blocks = tm // POOL_HALO
    kern = functools.partial(_pool_kernel, tm=tm, seq=seq)
    return pl.pallas_call(
        kern,
        grid=(t // tm,),
        in_specs=[
            pl.BlockSpec((tm, width), lambda i: (i, 0)),
            pl.BlockSpec((POOL_HALO, width), lambda i: (jnp.maximum(i * halo_blocks - 1, 0), 0)),
            pl.BlockSpec(w.shape, lambda i: (0, 0, 0)),
            pl.BlockSpec((1, width), lambda i: (0, 0)),
        ],
        out_specs=pl.BlockSpec((tm, width), lambda i: (i, 0)),
        out_shape=jax.ShapeDtypeStruct((t, width), BF16),
        scratch_shapes=[pltpu.VMEM((tm + POOL_HALO, width), F32)],
        compiler_params=_params(("parallel",)),
        name="pool",
    )(proj, proj, w, scale)


def _dil_kernel(q_ref, k_ref, v_ref, o_ref, qd_ref, kd_ref, vd_ref, qw_ref, kw_ref, vw_ref,
                m_ref, l_ref, acc_ref, *, seq):
    g = pl.program_id(2)
    wide_refs = (qw_ref, kw_ref, vw_ref)
    qb = DIL_QBLOCK
    nblocks = seq // qb

    @pl.when(g == 0)
    def _():
        m_ref[...] = jnp.full(m_ref.shape, MASK_VALUE, F32)
        l_ref[...] = jnp.zeros(l_ref.shape, F32)
        acc_ref[...] = jnp.zeros(acc_ref.shape, F32)
        kd_ref[pl.ds(0, qb), :] = jnp.zeros((qb, HEAD_DIM), BF16)
        vd_ref[pl.ds(0, qb), :] = jnp.zeros((qb, HEAD_DIM), BF16)

    row = lax.broadcasted_iota(jnp.int32, (qb, 2 * qb), 0)
    col = lax.broadcasted_iota(jnp.int32, (qb, 2 * qb), 1)
    band = jnp.logical_and(col >= row, col <= row + qb)
    own_block = col >= qb

    def run_group(dil):
        nblk = seq // dil // qb
        assert nblk & (nblk - 1) == 0
        shift = nblk.bit_length() - 1

        def positions(idx):
            r = lax.shift_right_logical(idx, shift)
            n = jnp.bitwise_and(idx, nblk - 1)
            start = r + dil * qb * n
            return (pl.ds(start, qb, stride=dil) if dil > 1 else pl.ds(pl.multiple_of(start, qb), qb)), n

        if dil > 1:
            def widen(idx, carry):
                rows = pl.ds(pl.multiple_of(idx * qb, qb), qb)
                for src_ref, wide_ref in zip((q_ref, k_ref, v_ref), wide_refs):
                    wide_ref[rows, :] = src_ref[rows, :].astype(F32)
                return carry

            lax.fori_loop(0, nblocks, widen, 0, unroll=4)
            sources = wide_refs
        else:
            sources = (q_ref, k_ref, v_ref)

        def gather(idx, carry):
            src, _ = positions(idx)
            qd_ref[pl.ds(pl.multiple_of(idx * qb, qb), qb), :] = sources[0][src, :].astype(BF16)
            dst = pl.ds(pl.multiple_of((idx + 1) * qb, qb), qb)
            kd_ref[dst, :] = sources[1][src, :].astype(BF16)
            vd_ref[dst, :] = sources[2][src, :].astype(BF16)
            return carry

        lax.fori_loop(0, nblocks, gather, 0, unroll=4)

        def window(idx):
            return pl.ds(pl.multiple_of(idx * qb, qb), 2 * qb)

        def block_scores(idx):
            q = qd_ref[pl.ds(pl.multiple_of(idx * qb, qb), qb), :]
            return _dot_nt(q, kd_ref[window(idx), :])

        def block_update(idx, s):
            cur, n = positions(idx)
            s = jnp.where(jnp.logical_and(band, jnp.logical_or(own_block, n > 0)), s, MASK_VALUE)
            m_old = m_ref[cur, :]
            m_new = jnp.maximum(m_old, jnp.max(s, axis=-1, keepdims=True))
            p = jnp.exp2(s - jnp.concatenate([m_new, m_new], axis=1))
            alpha = jnp.exp2(m_old - m_new)
            l_ref[cur, :] = alpha * l_ref[cur, :] + jnp.sum(p, axis=-1, keepdims=True)
            acc_ref[cur, :] = alpha * acc_ref[cur, :] + _dot(p.astype(BF16), vd_ref[window(idx), :])
            m_ref[cur, :] = m_new

        def body(it, carry):
            first = it * DIL_UNROLL
            scores = [block_scores(first + u) for u in range(DIL_UNROLL)]
            for u in range(DIL_UNROLL):
                block_update(first + u, scores[u])
            return carry

        lax.fori_loop(0, nblocks // DIL_UNROLL, body, 0)

    for gi, (_, dil) in enumerate(DIL_PATTERNS):
        pl.when(g == gi)(functools.partial(run_group, dil))

    @pl.when(g == len(DIL_PATTERNS) - 1)
    def _():
        o_ref[...] = (acc_ref[...] / l_ref[...]).astype(o_ref.dtype)


def _dilated(proj, *, batch, seq, col0):
    n_groups = len(DIL_PATTERNS)
    per = n_groups * ATT_HEADS
    c0 = col0 // HEAD_DIM
    kern = functools.partial(_dil_kernel, seq=seq)

    def spec(which):
        return pl.BlockSpec((seq, HEAD_DIM), lambda b, h, g: (b, c0 + which * per + g * ATT_HEADS + h))

    return pl.pallas_call(
        kern,
        grid=(batch, ATT_HEADS, n_groups),
        in_specs=[spec(0), spec(1), spec(2)],
        out_specs=pl.BlockSpec((seq, HEAD_DIM), lambda b, h, g: (b, h)),
        out_shape=jax.ShapeDtypeStruct((batch * seq, ATT_HEADS * HEAD_DIM), BF16),
        scratch_shapes=[
            pltpu.VMEM((seq, HEAD_DIM), BF16),
            pltpu.VMEM((seq + DIL_QBLOCK, HEAD_DIM), BF16),
            pltpu.VMEM((seq + DIL_QBLOCK, HEAD_DIM), BF16),
        ] + [pltpu.VMEM((seq, HEAD_DIM), F32)] * 6,
        compiler_params=_params(("parallel", "parallel", "arbitrary")),
        name="dilated",
    )(proj, proj, proj)


def _outproj_kernel(a_ref, b_ref, wa_ref, wb_ref, x_ref, g_ref, o_ref):
    y = _dot(a_ref[...], wa_ref[...]) + _dot(b_ref[...], wb_ref[...])
    o_ref[...] = x_ref[...] + _rms_norm(y, g_ref[...])


def _outproj(a, b, w, x, gain, *, tm=512):
    t, d = x.shape
    half = a.shape[1]
    assert w.shape == (2 * half, d) and t % tm == 0
    return pl.pallas_call(
        _outproj_kernel,
        grid=(t // tm,),
        in_specs=[
            pl.BlockSpec((tm, half), lambda i: (i, 0)),
            pl.BlockSpec((tm, half), lambda i: (i, 0)),
            pl.BlockSpec((half, d), lambda i: (0, 0)),
            pl.BlockSpec((half, d), lambda i: (1, 0)),
            pl.BlockSpec((tm, d), lambda i: (i, 0)),
            pl.BlockSpec((1, d), lambda i: (0, 0)),
        ],
        out_specs=pl.BlockSpec((tm, d), lambda i: (i, 0)),
        out_shape=jax.ShapeDtypeStruct((t, d), F32),
        compiler_params=_params(("parallel",)),
        name="outproj",
    )(a, b, w, w, x, gain)


def _ffn_kernel(x_next_ref, x_prev_ref, g_in_ref, g_out_ref, wg_ref, wu_ref, wd_ref, o_ref,
                hn_ref, hn_next_ref, acc_ref, first_ref, *, hidden_tiles, row_tiles):
    s = pl.program_id(0)
    i = s // hidden_tiles
    j = s - i * hidden_tiles
    closing = hidden_tiles * row_tiles

    def partial_product(h):
        gate = _dot(h, wg_ref[...])
        up = _dot(h, wu_ref[...])
        act = (gate * jax.nn.sigmoid(gate)) * up
        return _dot(act.astype(BF16), wd_ref[...])

    def finish_previous_tile():
        o_ref[...] = x_prev_ref[...] + _rms_norm(acc_ref[...], g_out_ref[...])

    @pl.when(s == 0)
    def _():
        h = _rms_norm(x_next_ref[...], g_in_ref[...]).astype(BF16)
        hn_ref[...] = h
        first_ref[...] = partial_product(h)

    @pl.when(jnp.logical_and(j == 0, jnp.logical_and(s > 0, s < closing)))
    def _():
        finish_previous_tile()
        h = hn_next_ref[...]
        hn_ref[...] = h
        first_ref[...] = partial_product(h)

    @pl.when(s == closing)
    def _():
        finish_previous_tile()

    @pl.when(j == 1)
    def _():
        acc_ref[...] = first_ref[...] + partial_product(hn_ref[...])

    @pl.when(jnp.logical_and(j > 1, j < hidden_tiles - 1))
    def _():
        acc_ref[...] += partial_product(hn_ref[...])

    @pl.when(j == hidden_tiles - 1)
    def _():
        acc_ref[...] += partial_product(hn_ref[...])
        hn_next_ref[...] = _rms_norm(x_next_ref[...], g_in_ref[...]).astype(BF16)


def _ffn(x, g_in, g_out, wg, wu, wd, layer, *, tm=512, tf=512):
    t, d = x.shape
    f = wg.shape[2]
    assert t % tm == 0 and f % tf == 0
    nj, n = f // tf, t // tm
    assert nj >= 3
    kern = functools.partial(_ffn_kernel, hidden_tiles=nj, row_tiles=n)

    def next_tile(s):
        return jnp.minimum((s + 1) // nj, n - 1), 0

    def previous_tile(s):
        return jnp.maximum((s - 1) // nj, 0), 0

    return pl.pallas_call(
        kern,
        grid=(n * nj + 1,),
        in_specs=[
            pl.BlockSpec((tm, d), next_tile),
            pl.BlockSpec((tm, d), previous_tile),
            pl.BlockSpec((1, d), lambda s: (0, 0)),
            pl.BlockSpec((1, d), lambda s: (0, 0)),
            pl.BlockSpec((None, d, tf), lambda s: (layer, 0, s % nj)),
            pl.BlockSpec((None, d, tf), lambda s: (layer, 0, s % nj)),
            pl.BlockSpec((None, tf, d), lambda s: (layer, s % nj, 0)),
        ],
        out_specs=pl.BlockSpec((tm, d), previous_tile),
        out_shape=jax.ShapeDtypeStruct((t, d), F32),
        scratch_shapes=[pltpu.VMEM((tm, d), BF16), pltpu.VMEM((tm, d), BF16),
                        pltpu.VMEM((tm, d), F32), pltpu.VMEM((tm, d), F32)],
        compiler_params=_params(("arbitrary",)),
        name="ffn",
    )(x, x, g_in, g_out, wg, wu, wd)


def _sb_suffix(z, tri, mask):
    tb = SB_BLOCK
    soft = jnp.where(z > SB_LINEAR_ABOVE, z, jnp.log(1.0 + jnp.exp2(z)) * LOG2E)
    if mask is not None:
        soft = jnp.where(mask, soft, 0.0)
    p16 = soft.astype(BF16)
    left, right = slice(0, tb), slice(tb, 2 * tb)
    suffix_l = _dot(p16[:, left], tri)
    suffix_r = _dot(p16[:, right], tri)
    total_l = jnp.sum(soft[:, left], axis=-1, keepdims=True)
    total_r = jnp.sum(soft[:, right], axis=-1, keepdims=True)
    return suffix_l, suffix_r, total_l, total_r


def _sb_weights(z, sums, carry, mask):
    tb = SB_BLOCK
    suffix_l, suffix_r, total_l, total_r = sums
    expo = jnp.concatenate([z[:, :tb] - suffix_l - (carry + total_r), z[:, tb:] - suffix_r - carry], axis=1)
    a = jnp.exp2(expo)
    if mask is not None:
        a = jnp.where(mask, a, 0.0)
    return a.astype(BF16), carry + total_r + total_l


def _sb_kernel(q_ref, k_ref, v_ref, tri_ref, o_ref, *, heads):
    qi = pl.program_id(2)
    tq = SB_TILE
    tri = tri_ref[...]
    row = lax.broadcasted_iota(jnp.int32, (tq, tq), 0)
    col = lax.broadcasted_iota(jnp.int32, (tq, tq), 1)
    strictly_past = col < row

    def head_cols(h):
        return slice(h * HEAD_DIM, (h + 1) * HEAD_DIM)

    qs = [q_ref[:, head_cols(h)] for h in range(heads)]

    def tiles(rows, accs, carries, mask):
        zs = [_dot_nt(qs[h], k_ref[rows, head_cols(h)]) for h in range(heads)]
        sums = [_sb_suffix(zs[h], tri, mask) for h in range(heads)]
        out = []
        for h in range(heads):
            a, carry = _sb_weights(zs[h], sums[h], carries[h], mask)
            out += [accs[h] + _dot(a, v_ref[rows, head_cols(h)]), carry]
        return out

    diag = pl.ds(pl.multiple_of(qi * tq, tq), tq)
    state = tiles(diag, [jnp.zeros((tq, HEAD_DIM), F32)] * heads, [jnp.zeros((tq, 1), F32)] * heads, strictly_past)

    def body(it, st):
        rows = pl.ds(pl.multiple_of((qi - 1 - it) * tq, tq), tq)
        return tuple(tiles(rows, st[0::2], st[1::2], None))

    state = lax.fori_loop(0, qi, body, tuple(state))
    for h in range(heads):
        o_ref[:, head_cols(h)] = state[2 * h].astype(o_ref.dtype)


def _stick_breaking(proj, *, batch, seq, col0, heads_per_step=2):
    tb, tq = SB_BLOCK, SB_TILE
    assert tq == 2 * tb and seq % tq == 0
    width = heads_per_step * HEAD_DIM
    c0 = col0 // width
    per = ATT_HEADS // heads_per_step
    nq = seq // tq
    tri = (lax.broadcasted_iota(jnp.int32, (tb, tb), 0) >= lax.broadcasted_iota(jnp.int32, (tb, tb), 1)).astype(BF16)
    kern = functools.partial(_sb_kernel, heads=heads_per_step)
    return pl.pallas_call(
        kern,
        grid=(batch, per, nq),
        in_specs=[
            pl.BlockSpec((tq, width), lambda b, h, i: (b * nq + i, c0 + h)),
            pl.BlockSpec((seq, width), lambda b, h, i: (b, c0 + per + h)),
            pl.BlockSpec((seq, width), lambda b, h, i: (b, c0 + 2 * per + h)),
            pl.BlockSpec((tb, tb), lambda b, h, i: (0, 0)),
        ],
        out_specs=pl.BlockSpec((tq, width), lambda b, h, i: (b * nq + i, h)),
        out_shape=jax.ShapeDtypeStruct((batch * seq, ATT_HEADS * HEAD_DIM), BF16),
        compiler_params=_params(("parallel", "parallel", "arbitrary")),
        name="stick_breaking",
    )(proj, proj, proj, tri)


def _moba_setup(k, kaug_ref, kmean_ref, *, seq):
    blk = MOBA_BLOCK
    nb = kmean_ref.shape[1]
    lane = lax.broadcasted_iota(jnp.int32, (blk, HEAD_DIM), 1)
    for n in range(seq // blk):
        kaug_ref[n * blk:(n + 1) * blk, :HEAD_DIM] = k[n * blk:(n + 1) * blk, :]
        kaug_ref[n * blk:(n + 1) * blk, HEAD_DIM:] = jnp.where(lane == n, 1.0, 0.0).astype(BF16)
    key = lax.broadcasted_iota(jnp.int32, (nb, seq), 1)
    first = lax.broadcasted_iota(jnp.int32, (nb, seq), 0) * blk
    in_block = jnp.logical_and(key >= first, key < first + blk)
    mean = _dot(jnp.where(in_block, 1.0 / blk, 0.0).astype(BF16), k)
    hi = mean.astype(BF16)
    mid = (mean - hi.astype(F32)).astype(BF16)
    lo = (mean - hi.astype(F32) - mid.astype(F32)).astype(BF16)
    kmean_ref[0] = hi
    kmean_ref[1] = mid
    kmean_ref[2] = lo


def _moba_select(q, qi, kmean_ref):
    blk = MOBA_BLOCK
    nb = kmean_ref.shape[1]
    gate = _dot_nt(kmean_ref[0], q) + _dot_nt(kmean_ref[1], q) + _dot_nt(kmean_ref[2], q)
    blk_id = lax.broadcasted_iota(jnp.int32, (nb, blk), 0)
    past = blk_id < qi
    gate = jnp.where(past, gate, -jnp.inf)
    beaten = jnp.zeros((nb, blk), F32)
    for m in range(nb):
        gm = gate[m:m + 1, :]
        ahead = jnp.logical_or(gm > gate, jnp.logical_and(gm == gate, blk_id > m))
        beaten = beaten + jnp.where(ahead, 1.0, 0.0)
    keep = jnp.logical_or(jnp.logical_and(past, beaten < MOBA_TOPK), blk_id == qi)
    bias = jnp.where(keep, 0.0, MOBA_UNSELECTED)
    bias = jnp.concatenate([bias, jnp.zeros((HEAD_DIM - nb, blk), F32)], axis=0)
    return jnp.concatenate([q, bias.T.astype(BF16)], axis=1)


def _moba_kernel(q_ref, k_ref, v_ref, o_ref, kaug_ref, kmean_ref, *, seq, heads):
    qi = pl.program_id(2)
    blk = MOBA_BLOCK
    pair = 2 * blk

    def head_cols(h):
        return slice(h * HEAD_DIM, (h + 1) * HEAD_DIM)

    @pl.when(qi == 0)
    def _():
        for h in range(heads):
            _moba_setup(k_ref[:, head_cols(h)], kaug_ref.at[h], kmean_ref.at[h], seq=seq)

    qa = [_moba_select(q_ref[:, head_cols(h)], qi, kmean_ref.at[h]) for h in range(heads)]

    def scores(h, j):
        rows = pl.ds(pl.multiple_of(j * pair, pair), pair)
        return _dot_nt(qa[h], kaug_ref[h, rows, :]), v_ref[rows, head_cols(h)]

    last = lax.shift_right_logical(qi, 1)
    row = lax.broadcasted_iota(jnp.int32, (blk, pair), 0)
    col = lax.broadcasted_iota(jnp.int32, (blk, pair), 1)
    causal = col <= row + (qi - 2 * last) * blk
    state = []
    for h in range(heads):
        s, v = scores(h, last)
        s = jnp.where(causal, s, MASK_VALUE)
        m = jnp.max(s, axis=-1, keepdims=True)
        p = jnp.exp2(s - m)
        state += [m, jnp.sum(p, axis=-1, keepdims=True), _dot(p.astype(BF16), v)]

    def body(j, st):
        out = []
        nxt = scores(0, j)
        for h in range(heads):
            s, v = nxt
            if h + 1 < heads:
                nxt = scores(h + 1, j)
            m_old, l_old, acc = st[3 * h:3 * h + 3]
            m_new = jnp.maximum(m_old, jnp.max(s, axis=-1, keepdims=True))
            alpha = jnp.exp2(m_old - m_new)
            p = jnp.exp2(s - m_new)
            out += [m_new, alpha * l_old + jnp.sum(p, axis=-1, keepdims=True),
                    alpha * acc + _dot(p.astype(BF16), v)]
        return tuple(out)

    state = lax.fori_loop(0, last, body, tuple(state))
    for h in range(heads):
        o_ref[:, head_cols(h)] = (state[3 * h + 2] / state[3 * h + 1]).astype(o_ref.dtype)


def _moba(proj, *, batch, seq, col0, heads_per_step=4):
    blk = MOBA_BLOCK
    nq = seq // blk
    assert nq % 2 == 0
    nb_pad = -(-nq // 16) * 16
    assert nb_pad <= HEAD_DIM
    width = heads_per_step * HEAD_DIM
    c0 = col0 // width
    per = ATT_HEADS // heads_per_step
    kern = functools.partial(_moba_kernel, seq=seq, heads=heads_per_step)
    return pl.pallas_call(
        kern,
        grid=(batch, per, nq),
        in_specs=[
            pl.BlockSpec((blk, width), lambda b, h, i: (b * nq + i, c0 + h)),
            pl.BlockSpec((seq, width), lambda b, h, i: (b, c0 + per + h)),
            pl.BlockSpec((seq, width), lambda b, h, i: (b, c0 + 2 * per + h)),
        ],
        out_specs=pl.BlockSpec((blk, width), lambda b, h, i: (b * nq + i, h)),
        out_shape=jax.ShapeDtypeStruct((batch * seq, ATT_HEADS * HEAD_DIM), BF16),
        scratch_shapes=[
            pltpu.VMEM((heads_per_step, seq, 2 * HEAD_DIM), BF16),
            pltpu.VMEM((heads_per_step, 3, nb_pad, HEAD_DIM), BF16),
        ],
        compiler_params=_params(("parallel", "parallel", "arbitrary")),
        name="moba",
    )(proj, proj, proj)


def _rope_tables(seq):
    inv = 1.0 / (ROPE_THETA ** (jnp.arange(0, HEAD_DIM, 2, dtype=F32) / HEAD_DIM))
    ang = jnp.arange(seq, dtype=F32)[:, None] * inv[None, :]
    cos, sin = jnp.cos(ang), jnp.sin(ang)
    return jnp.concatenate([cos, cos], axis=-1), jnp.concatenate([-sin, sin], axis=-1)


def _colscale(n, q_ranges):
    cs = jnp.ones((1, n), F32)
    for lo, hi in q_ranges:
        cs = cs.at[:, lo:hi].set(Q_SCALE)
    return cs


def kernel(x, norm_gains, w_in_ab, pool_w, pool_scale, w_out_ab, w_in_cd, w_out_cd, ffn_gate, ffn_up, ffn_down):
    batch, seq, d = x.shape
    att = ATT_HEADS * HEAD_DIM
    pool_width = len(POOL_WINDOWS) * POOL_GROUP
    n_groups = len(DIL_PATTERNS)
    cos, sin = _rope_tables(seq)
    gains = norm_gains.reshape(norm_gains.shape[0], 4, 1, d)
    xs = x.reshape(batch * seq, d)

    qkv0 = pool_width
    proj = _inproj(xs, gains[0, 0], w_in_ab[0].astype(BF16),
                   _colscale(w_in_ab.shape[2], [(qkv0, qkv0 + n_groups * att)]), cos, sin, seq=seq,
                   rope_cols=(qkv0, qkv0 + 2 * n_groups * att))
    a_out = _pool(proj, pool_w[0].astype(BF16), pool_scale[0].reshape(1, pool_width), seq=seq)
    b_out = _dilated(proj, batch=batch, seq=seq, col0=qkv0)
    xs = _outproj(a_out, b_out, w_out_ab[0].astype(BF16), xs, gains[0, 1])
    ffn_w = (ffn_gate.astype(BF16), ffn_up.astype(BF16), ffn_down.astype(BF16))
    xs = _ffn(xs, gains[0, 2], gains[0, 3], *ffn_w, 0)

    moba0 = 3 * att
    proj = _inproj(xs, gains[1, 0], w_in_cd[0].astype(BF16),
                   _colscale(w_in_cd.shape[2], [(0, att), (moba0, moba0 + att)]), cos, sin, seq=seq,
                   rope_cols=(moba0, moba0 + 2 * att))
    c_out = _stick_breaking(proj, batch=batch, seq=seq, col0=0)
    d_out = _moba(proj, batch=batch, seq=seq, col0=moba0)
    xs = _outproj(c_out, d_out, w_out_cd[0].astype(BF16), xs, gains[1, 1])
    xs = _ffn(xs, gains[1, 2], gains[1, 3], *ffn_w, 1)
    return xs.reshape(batch, seq, d)
```

```python
import functools

import jax
import jax.numpy as jnp
from jax import lax
from jax.experimental import pallas as pl
from jax.experimental.pallas import tpu as pltpu

F32 = jnp.float32
BF16 = jnp.bfloat16

HEAD_DIM = 128
EPS = 1e-6
ROPE_THETA = 10000.0
POOL_WINDOWS = (2, 4, 8, 16)
POOL_GROUP = 256
POOL_HALO = 16
DIL_PATTERNS = ((128, 1), (512, 4), (2048, 16))
DIL_QBLOCK = 128
DIL_UNROLL = 8
ATT_HEADS = 8
MOBA_BLOCK = 256
MOBA_TOPK = 3
SB_BLOCK = 256
SB_TILE = 2 * SB_BLOCK
SB_LINEAR_ABOVE = 64.0
LOG2E = 1.4426950408889634
Q_SCALE = LOG2E * HEAD_DIM ** -0.5
MASK_VALUE = -1e30
MOBA_UNSELECTED = -30000.0

V7X_VMEM_BYTES = 64 * 1024 * 1024
VMEM_LIMIT = V7X_VMEM_BYTES - 8 * 1024 * 1024


def _params(semantics):
    return pltpu.CompilerParams(dimension_semantics=semantics, vmem_limit_bytes=VMEM_LIMIT)


def _dot(a, b):
    return jnp.dot(a, b, preferred_element_type=F32)


def _dot_nt(a, b):
    return lax.dot_general(a, b, (((1,), (1,)), ((), ())), preferred_element_type=F32)


def _rms_norm(x, gain):
    return x * lax.rsqrt(jnp.mean(x * x, axis=-1, keepdims=True) + EPS) * gain


def _inproj_kernel(x_ref, g_ref, w_ref, cs_ref, cos_ref, sin_ref, o_ref, hn_ref, *, rope_lo, rope_hi):
    j = pl.program_id(1)

    @pl.when(j == 0)
    def _():
        hn_ref[...] = _rms_norm(x_ref[...], g_ref[...]).astype(BF16)

    acc = _dot(hn_ref[...], w_ref[...]) * cs_ref[...]
    is_rope = jnp.logical_and(j >= rope_lo, j < rope_hi)
    cos = jnp.where(is_rope, cos_ref[...], 1.0)
    sin = jnp.where(is_rope, sin_ref[...], 0.0)
    for c in range(acc.shape[1] // HEAD_DIM):
        blk = acc[:, c * HEAD_DIM:(c + 1) * HEAD_DIM]
        rot = pltpu.roll(blk, HEAD_DIM // 2, 1)
        o_ref[:, c * HEAD_DIM:(c + 1) * HEAD_DIM] = (blk * cos + rot * sin).astype(o_ref.dtype)


def _inproj(x, gain, w, colscale, cos, sin, *, seq, rope_cols, tm=1024, tn=1024):
    t, d = x.shape
    n = w.shape[1]
    assert t % tm == 0 and n % tn == 0 and seq % tm == 0
    assert rope_cols[0] % tn == 0 and rope_cols[1] % tn == 0
    pos_blocks = seq // tm
    kern = functools.partial(_inproj_kernel, rope_lo=rope_cols[0] // tn, rope_hi=rope_cols[1] // tn)
    return pl.pallas_call(
        kern,
        grid=(t // tm, n // tn),
        in_specs=[
            pl.BlockSpec((tm, d), lambda i, j: (i, 0)),
            pl.BlockSpec((1, d), lambda i, j: (0, 0)),
            pl.BlockSpec((d, tn), lambda i, j: (0, j)),
            pl.BlockSpec((1, tn), lambda i, j: (0, j)),
            pl.BlockSpec((tm, HEAD_DIM), lambda i, j: (i % pos_blocks, 0)),
            pl.BlockSpec((tm, HEAD_DIM), lambda i, j: (i % pos_blocks, 0)),
        ],
        out_specs=pl.BlockSpec((tm, tn), lambda i, j: (i, j)),
        out_shape=jax.ShapeDtypeStruct((t, n), BF16),
        scratch_shapes=[pltpu.VMEM((tm, d), BF16)],
        compiler_params=_params(("parallel", "arbitrary")),
        name="inproj",
    )(x, gain, w, colscale, cos, sin)


def _pool_kernel(u_ref, halo_ref, w_ref, sc_ref, o_ref, ext_ref, *, tm, seq):
    pos0 = (pl.program_id(0) * tm) % seq
    ext_ref[pl.ds(POOL_HALO, tm), :] = u_ref[...].astype(F32)
    ext_ref[pl.ds(0, POOL_HALO), :] = jnp.where(pos0 == 0, 0.0, halo_ref[...].astype(F32))
    pos = pos0 + lax.broadcasted_iota(jnp.int32, (tm, 1), 0)
    for g, w in enumerate(POOL_WINDOWS):
        cols = pl.ds(g * POOL_GROUP, POOL_GROUP)
        cur = ext_ref[pl.ds(POOL_HALO, tm), cols]
        tot = cur
        for back in range(1, w):
            tot = tot + ext_ref[pl.ds(POOL_HALO - back, tm), cols]
        cnt = jnp.minimum(pos + 1, w).astype(F32)
        pooled = tot / cnt - cur
        y = _dot(pooled.astype(BF16), w_ref[g])
        o_ref[:, g * POOL_GROUP:(g + 1) * POOL_GROUP] = (
            y * sc_ref[:, g * POOL_GROUP:(g + 1) * POOL_GROUP]).astype(o_ref.dtype)


def _pool(proj, w, scale, *, seq, tm=512):
    t = proj.shape[0]
    width = len(POOL_WINDOWS) * POOL_GROUP
    assert seq % tm == 0 and tm % POOL_HALO == 0
    halo_blocks = tm // POOL_HALO
    kern = functools.partial(_pool_kernel, tm=tm, seq=seq)
    return pl.pallas_call(
        kern,
        grid=(t // tm,),
        in_specs=[
            pl.BlockSpec((tm, width), lambda i: (i, 0)),
            pl.BlockSpec((POOL_HALO, width), lambda i: (jnp.maximum(i * halo_blocks - 1, 0), 0)),
            pl.BlockSpec(w.shape, lambda i: (0, 0, 0)),
            pl.BlockSpec((1, width), lambda i: (0, 0)),
        ],
        out_specs=pl.BlockSpec((tm, width), lambda i: (i, 0)),
        out_shape=jax.ShapeDtypeStruct((t, width), BF16),
        scratch_shapes=[pltpu.VMEM((tm + POOL_HALO, width), F32)],
        compiler_params=_params(("parallel",)),
        name="pool",
    )(proj, proj, w, scale)


def _dil_kernel(q_ref, k_ref, v_ref, o_ref, qd_ref, kd_ref, vd_ref, qw_ref, kw_ref, vw_ref,
                m_ref, l_ref, acc_ref, *, seq):
    g = pl.program_id(2)
    wide_refs = (qw_ref, kw_ref, vw_ref)
    qb = DIL_QBLOCK
    nblocks = seq // qb

    @pl.when(g == 0)
    def _():
        m_ref[...] = jnp.full(m_ref.shape, MASK_VALUE, F32)
        l_ref[...] = jnp.zeros(l_ref.shape, F32)
        acc_ref[...] = jnp.zeros(acc_ref.shape, F32)
        kd_ref[pl.ds(0, qb), :] = jnp.zeros((qb, HEAD_DIM), BF16)
        vd_ref[pl.ds(0, qb), :] = jnp.zeros((qb, HEAD_DIM), BF16)

    row = lax.broadcasted_iota(jnp.int32, (qb, 2 * qb), 0)
    col = lax.broadcasted_iota(jnp.int32, (qb, 2 * qb), 1)
    band = jnp.logical_and(col >= row, col <= row + qb)
    own_block = col >= qb

    def run_group(dil):
        nblk = seq // dil // qb
        assert nblk & (nblk - 1) == 0
        shift = nblk.bit_length() - 1

        def positions(idx):
            r = lax.shift_right_logical(idx, shift)
            n = jnp.bitwise_and(idx, nblk - 1)
            start = r + dil * qb * n
            return (pl.ds(start, qb, stride=dil) if dil > 1 else pl.ds(pl.multiple_of(start, qb), qb)), n

        if dil > 1:
            def widen(idx, carry):
                rows = pl.ds(pl.multiple_of(idx * qb, qb), qb)
                for src_ref, wide_ref in zip((q_ref, k_ref, v_ref), wide_refs):
                    wide_ref[rows, :] = src_ref[rows, :].astype(F32)
                return carry

            lax.fori_loop(0, nblocks, widen, 0, unroll=4)
            sources = wide_refs
        else:
            sources = (q_ref, k_ref, v_ref)

        def gather(idx, carry):
            src, _ = positions(idx)
            qd_ref[pl.ds(pl.multiple_of(idx * qb, qb), qb), :] = sources[0][src, :].astype(BF16)
            dst = pl.ds(pl.multiple_of((idx + 1) * qb, qb), qb)
            kd_ref[dst, :] = sources[1][src, :].astype(BF16)
            vd_ref[dst, :] = sources[2][src, :].astype(BF16)
            return carry

        lax.fori_loop(0, nblocks, gather, 0, unroll=4)

        def window(idx):
            return pl.ds(pl.multiple_of(idx * qb, qb), 2 * qb)

        def block_scores(idx):
            q = qd_ref[pl.ds(pl.multiple_of(idx * qb, qb), qb), :]
            return _dot_nt(q, kd_ref[window(idx), :])

        def block_update(idx, s):
            cur, n = positions(idx)
            s = jnp.where(jnp.logical_and(band, jnp.logical_or(own_block, n > 0)), s, MASK_VALUE)
            m_old = m_ref[cur, :]
            m_new = jnp.maximum(m_old, jnp.max(s, axis=-1, keepdims=True))
            p = jnp.exp2(s - jnp.concatenate([m_new, m_new], axis=1))
            alpha = jnp.exp2(m_old - m_new)
            l_ref[cur, :] = alpha * l_ref[cur, :] + jnp.sum(p, axis=-1, keepdims=True)
            acc_ref[cur, :] = alpha * acc_ref[cur, :] + _dot(p.astype(BF16), vd_ref[window(idx), :])
            m_ref[cur, :] = m_new

        def body(it, carry):
            first = it * DIL_UNROLL
            scores = [block_scores(first + u) for u in range(DIL_UNROLL)]
            for u in range(DIL_UNROLL):
                block_update(first + u, scores[u])
            return carry

        lax.fori_loop(0, nblocks // DIL_UNROLL, body, 0)

    for gi, (_, dil) in enumerate(DIL_PATTERNS):
        pl.when(g == gi)(functools.partial(run_group, dil))

    @pl.when(g == len(DIL_PATTERNS) - 1)
    def _():
        o_ref[...] = (acc_ref[...] / l_ref[...]).astype(o_ref.dtype)


def _dilated(proj, *, batch, seq, col0):
    n_groups = len(DIL_PATTERNS)
    per = n_groups * ATT_HEADS
    c0 = col0 // HEAD_DIM
    kern = functools.partial(_dil_kernel, seq=seq)

    def spec(which):
        return pl.BlockSpec((seq, HEAD_DIM), lambda b, h, g: (b, c0 + which * per + g * ATT_HEADS + h))

    return pl.pallas_call(
        kern,
        grid=(batch, ATT_HEADS, n_groups),
        in_specs=[spec(0), spec(1), spec(2)],
        out_specs=pl.BlockSpec((seq, HEAD_DIM), lambda b, h, g: (b, h)),
        out_shape=jax.ShapeDtypeStruct((batch * seq, ATT_HEADS * HEAD_DIM), BF16),
        scratch_shapes=[
            pltpu.VMEM((seq, HEAD_DIM), BF16),
            pltpu.VMEM((seq + DIL_QBLOCK, HEAD_DIM), BF16),
            pltpu.VMEM((seq + DIL_QBLOCK, HEAD_DIM), BF16),
        ] + [pltpu.VMEM((seq, HEAD_DIM), F32)] * 6,
        compiler_params=_params(("parallel", "parallel", "arbitrary")),
        name="dilated",
    )(proj, proj, proj)


def _outproj_kernel(a_ref, b_ref, wa_ref, wb_ref, x_ref, g_ref, o_ref):
    y = _dot(a_ref[...], wa_ref[...]) + _dot(b_ref[...], wb_ref[...])
    o_ref[...] = x_ref[...] + _rms_norm(y, g_ref[...])


def _outproj(a, b, w, x, gain, *, tm=512):
    t, d = x.shape
    half = a.shape[1]
    assert w.shape == (2 * half, d) and t % tm == 0
    return pl.pallas_call(
        _outproj_kernel,
        grid=(t // tm,),
        in_specs=[
            pl.BlockSpec((tm, half), lambda i: (i, 0)),
            pl.BlockSpec((tm, half), lambda i: (i, 0)),
            pl.BlockSpec((half, d), lambda i: (0, 0)),
            pl.BlockSpec((half, d), lambda i: (1, 0)),
            pl.BlockSpec((tm, d), lambda i: (i, 0)),
            pl.BlockSpec((1, d), lambda i: (0, 0)),
        ],
        out_specs=pl.BlockSpec((tm, d), lambda i: (i, 0)),
        out_shape=jax.ShapeDtypeStruct((t, d), F32),
        compiler_params=_params(("parallel",)),
        name="outproj",
    )(a, b, w, w, x, gain)


def _ffn_kernel(x_ref, g_in_ref, g_out_ref, wg_ref, wu_ref, wd_ref, o_ref, hn_ref, act_ref, acc_ref):
    j = pl.program_id(1)
    drain = pl.num_programs(1) - 1

    def activation():
        h = hn_ref[...]
        gate = _dot(h, wg_ref[...])
        up = _dot(h, wu_ref[...])
        return ((gate * jax.nn.sigmoid(gate)) * up).astype(BF16)

    @pl.when(j == 0)
    def _():
        hn_ref[...] = _rms_norm(x_ref[...], g_in_ref[...]).astype(BF16)
        acc_ref[...] = jnp.zeros(acc_ref.shape, F32)
        act_ref[...] = activation()

    @pl.when(jnp.logical_and(j > 0, j < drain))
    def _():
        acc_ref[...] += _dot(act_ref[...], wd_ref[...])
        act_ref[...] = activation()

    @pl.when(j == drain)
    def _():
        acc = acc_ref[...] + _dot(act_ref[...], wd_ref[...])
        o_ref[...] = x_ref[...] + _rms_norm(acc, g_out_ref[...])


def _ffn(x, g_in, g_out, wg, wu, wd, layer, *, tm=512, tf=512):
    t, d = x.shape
    f = wg.shape[2]
    assert t % tm == 0 and f % tf == 0
    nj = f // tf
    return pl.pallas_call(
        _ffn_kernel,
        grid=(t // tm, nj + 1),
        in_specs=[
            pl.BlockSpec((tm, d), lambda i, j: (i, 0)),
            pl.BlockSpec((1, d), lambda i, j: (0, 0)),
            pl.BlockSpec((1, d), lambda i, j: (0, 0)),
            pl.BlockSpec((None, d, tf), lambda i, j: (layer, 0, jnp.minimum(j, nj - 1))),
            pl.BlockSpec((None, d, tf), lambda i, j: (layer, 0, jnp.minimum(j, nj - 1))),
            pl.BlockSpec((None, tf, d), lambda i, j: (layer, jnp.maximum(j - 1, 0), 0)),
        ],
        out_specs=pl.BlockSpec((tm, d), lambda i, j: (i, 0)),
        out_shape=jax.ShapeDtypeStruct((t, d), F32),
        scratch_shapes=[pltpu.VMEM((tm, d), BF16), pltpu.VMEM((tm, tf), BF16), pltpu.VMEM((tm, d), F32)],
        compiler_params=_params(("parallel", "arbitrary")),
        name="ffn",
    )(x, g_in, g_out, wg, wu, wd)


def _sb_suffix(z, tri, mask):
    tb = SB_BLOCK
    soft = jnp.where(z > SB_LINEAR_ABOVE, z, jnp.log(1.0 + jnp.exp2(z)) * LOG2E)
    if mask is not None:
        soft = jnp.where(mask, soft, 0.0)
    p16 = soft.astype(BF16)
    left, right = slice(0, tb), slice(tb, 2 * tb)
    suffix_l = _dot(p16[:, left], tri)
    suffix_r = _dot(p16[:, right], tri)
    total_l = jnp.sum(soft[:, left], axis=-1, keepdims=True)
    total_r = jnp.sum(soft[:, right], axis=-1, keepdims=True)
    return suffix_l, suffix_r, total_l, total_r


def _sb_weights(z, sums, carry, mask):
    tb = SB_BLOCK
    suffix_l, suffix_r, total_l, total_r = sums
    expo = jnp.concatenate([z[:, :tb] - suffix_l - (carry + total_r), z[:, tb:] - suffix_r - carry], axis=1)
    a = jnp.exp2(expo)
    if mask is not None:
        a = jnp.where(mask, a, 0.0)
    return a.astype(BF16), carry + total_r + total_l


def _sb_kernel(q_ref, k_ref, v_ref, tri_ref, o_ref, *, heads):
    qi = pl.program_id(2)
    tq = SB_TILE
    tri = tri_ref[...]
    row = lax.broadcasted_iota(jnp.int32, (tq, tq), 0)
    col = lax.broadcasted_iota(jnp.int32, (tq, tq), 1)
    strictly_past = col < row

    def head_cols(h):
        return slice(h * HEAD_DIM, (h + 1) * HEAD_DIM)

    qs = [q_ref[:, head_cols(h)] for h in range(heads)]

    def tiles(rows, accs, carries, mask):
        zs = [_dot_nt(qs[h], k_ref[rows, head_cols(h)]) for h in range(heads)]
        sums = [_sb_suffix(zs[h], tri, mask) for h in range(heads)]
        out = []
        for h in range(heads):
            a, carry = _sb_weights(zs[h], sums[h], carries[h], mask)
            out += [accs[h] + _dot(a, v_ref[rows, head_cols(h)]), carry]
        return out

    diag = pl.ds(pl.multiple_of(qi * tq, tq), tq)
    state = tiles(diag, [jnp.zeros((tq, HEAD_DIM), F32)] * heads, [jnp.zeros((tq, 1), F32)] * heads, strictly_past)

    def body(it, st):
        rows = pl.ds(pl.multiple_of((qi - 1 - it) * tq, tq), tq)
        return tuple(tiles(rows, st[0::2], st[1::2], None))

    state = lax.fori_loop(0, qi, body, tuple(state))
    for h in range(heads):
        o_ref[:, head_cols(h)] = state[2 * h].astype(o_ref.dtype)


def _stick_breaking(proj, *, batch, seq, col0, heads_per_step=2):
    tb, tq = SB_BLOCK, SB_TILE
    assert tq == 2 * tb and seq % tq == 0
    width = heads_per_step * HEAD_DIM
    c0 = col0 // width
    per = ATT_HEADS // heads_per_step
    nq = seq // tq
    tri = (lax.broadcasted_iota(jnp.int32, (tb, tb), 0) >= lax.broadcasted_iota(jnp.int32, (tb, tb), 1)).astype(BF16)
    kern = functools.partial(_sb_kernel, heads=heads_per_step)
    return pl.pallas_call(
        kern,
        grid=(batch, per, nq),
        in_specs=[
            pl.BlockSpec((tq, width), lambda b, h, i: (b * nq + i, c0 + h)),
            pl.BlockSpec((seq, width), lambda b, h, i: (b, c0 + per + h)),
            pl.BlockSpec((seq, width), lambda b, h, i: (b, c0 + 2 * per + h)),
            pl.BlockSpec((tb, tb), lambda b, h, i: (0, 0)),
        ],
        out_specs=pl.BlockSpec((tq, width), lambda b, h, i: (b * nq + i, h)),
        out_shape=jax.ShapeDtypeStruct((batch * seq, ATT_HEADS * HEAD_DIM), BF16),
        compiler_params=_params(("parallel", "parallel", "arbitrary")),
        name="stick_breaking",
    )(proj, proj, proj, tri)


def _moba_setup(k, kaug_ref, kmean_ref, *, seq):
    blk = MOBA_BLOCK
    nb = kmean_ref.shape[1]
    lane = lax.broadcasted_iota(jnp.int32, (blk, HEAD_DIM), 1)
    for n in range(seq // blk):
        kaug_ref[n * blk:(n + 1) * blk, :HEAD_DIM] = k[n * blk:(n + 1) * blk, :]
        kaug_ref[n * blk:(n + 1) * blk, HEAD_DIM:] = jnp.where(lane == n, 1.0, 0.0).astype(BF16)
    key = lax.broadcasted_iota(jnp.int32, (nb, seq), 1)
    first = lax.broadcasted_iota(jnp.int32, (nb, seq), 0) * blk
    in_block = jnp.logical_and(key >= first, key < first + blk)
    mean = _dot(jnp.where(in_block, 1.0 / blk, 0.0).astype(BF16), k)
    hi = mean.astype(BF16)
    mid = (mean - hi.astype(F32)).astype(BF16)
    lo = (mean - hi.astype(F32) - mid.astype(F32)).astype(BF16)
    kmean_ref[0] = hi
    kmean_ref[1] = mid
    kmean_ref[2] = lo


def _moba_select(q, qi, kmean_ref):
    blk = MOBA_BLOCK
    nb = kmean_ref.shape[1]
    gate = _dot_nt(kmean_ref[0], q) + _dot_nt(kmean_ref[1], q) + _dot_nt(kmean_ref[2], q)
    blk_id = lax.broadcasted_iota(jnp.int32, (nb, blk), 0)
    past = blk_id < qi
    gate = jnp.where(past, gate, -jnp.inf)
    beaten = jnp.zeros((nb, blk), F32)
    for m in range(nb):
        gm = gate[m:m + 1, :]
        ahead = jnp.logical_or(gm > gate, jnp.logical_and(gm == gate, blk_id > m))
        beaten = beaten + jnp.where(ahead, 1.0, 0.0)
    keep = jnp.logical_or(jnp.logical_and(past, beaten < MOBA_TOPK), blk_id == qi)
    bias = jnp.where(keep, 0.0, MOBA_UNSELECTED)
    bias = jnp.concatenate([bias, jnp.zeros((HEAD_DIM - nb, blk), F32)], axis=0)
    return jnp.concatenate([q, bias.T.astype(BF16)], axis=1)


def _moba_kernel(q_ref, k_ref, v_ref, o_ref, kaug_ref, kmean_ref, *, seq, heads):
    qi = pl.program_id(2)
    blk = MOBA_BLOCK
    pair = 2 * blk

    def head_cols(h):
        return slice(h * HEAD_DIM, (h + 1) * HEAD_DIM)

    @pl.when(qi == 0)
    def _():
        for h in range(heads):
            _moba_setup(k_ref[:, head_cols(h)], kaug_ref.at[h], kmean_ref.at[h], seq=seq)

    qa = [_moba_select(q_ref[:, head_cols(h)], qi, kmean_ref.at[h]) for h in range(heads)]

    def scores(h, j):
        rows = pl.ds(pl.multiple_of(j * pair, pair), pair)
        return _dot_nt(qa[h], kaug_ref[h, rows, :]), v_ref[rows, head_cols(h)]

    last = lax.shift_right_logical(qi, 1)
    row = lax.broadcasted_iota(jnp.int32, (blk, pair), 0)
    col = lax.broadcasted_iota(jnp.int32, (blk, pair), 1)
    causal = col <= row + (qi - 2 * last) * blk
    state = []
    for h in range(heads):
        s, v = scores(h, last)
        s = jnp.where(causal, s, MASK_VALUE)
        m = jnp.max(s, axis=-1, keepdims=True)
        p = jnp.exp2(s - m)
        state += [m, jnp.sum(p, axis=-1, keepdims=True), _dot(p.astype(BF16), v)]

    def body(j, st):
        out = []
        nxt = scores(0, j)
        for h in range(heads):
            s, v = nxt
            if h + 1 < heads:
                nxt = scores(h + 1, j)
            m_old, l_old, acc = st[3 * h:3 * h + 3]
            m_new = jnp.maximum(m_old, jnp.max(s, axis=-1, keepdims=True))
            alpha = jnp.exp2(m_old - m_new)
            p = jnp.exp2(s - m_new)
            out += [m_new, alpha * l_old + jnp.sum(p, axis=-1, keepdims=True),
                    alpha * acc + _dot(p.astype(BF16), v)]
        return tuple(out)

    state = lax.fori_loop(0, last, body, tuple(state))
    for h in range(heads):
        o_ref[:, head_cols(h)] = (state[3 * h + 2] / state[3 * h + 1]).astype(o_ref.dtype)


def _moba(proj, *, batch, seq, col0, heads_per_step=4):
    blk = MOBA_BLOCK
    nq = seq // blk
    assert nq % 2 == 0
    nb_pad = -(-nq // 16) * 16
    assert nb_pad <= HEAD_DIM
    width = heads_per_step * HEAD_DIM
    c0 = col0 // width
    per = ATT_HEADS // heads_per_step
    kern = functools.partial(_moba_kernel, seq=seq, heads=heads_per_step)
    return pl.pallas_call(
        kern,
        grid=(batch, per, nq),
        in_specs=[
            pl.BlockSpec((blk, width), lambda b, h, i: (b * nq + i, c0 + h)),
            pl.BlockSpec((seq, width), lambda b, h, i: (b, c0 + per + h)),
            pl.BlockSpec((seq, width), lambda b, h, i: (b, c0 + 2 * per + h)),
        ],
        out_specs=pl.BlockSpec((blk, width), lambda b, h, i: (b * nq + i, h)),
        out_shape=jax.ShapeDtypeStruct((batch * seq, ATT_HEADS * HEAD_DIM), BF16),
        scratch_shapes=[
            pltpu.VMEM((heads_per_step, seq, 2 * HEAD_DIM), BF16),
            pltpu.VMEM((heads_per_step, 3, nb_pad, HEAD_DIM), BF16),
        ],
        compiler_params=_params(("parallel", "parallel", "arbitrary")),
        name="moba",
    )(proj, proj, proj)


def _rope_tables(seq):
    inv = 1.0 / (ROPE_THETA ** (jnp.arange(0, HEAD_DIM, 2, dtype=F32) / HEAD_DIM))
    ang = jnp.arange(seq, dtype=F32)[:, None] * inv[None, :]
    cos, sin = jnp.cos(ang), jnp.sin(ang)
    return jnp.concatenate([cos, cos], axis=-1), jnp.concatenate([-sin, sin], axis=-1)


def _colscale(n, q_ranges):
    cs = jnp.ones((1, n), F32)
    for lo, hi in q_ranges:
        cs = cs.at[:, lo:hi].set(Q_SCALE)
    return cs


def kernel(x, norm_gains, w_in_ab, pool_w, pool_scale, w_out_ab, w_in_cd, w_out_cd, ffn_gate, ffn_up, ffn_down):
    batch, seq, d = x.shape
    att = ATT_HEADS * HEAD_DIM
    pool_width = len(POOL_WINDOWS) * POOL_GROUP
    n_groups = len(DIL_PATTERNS)
    cos, sin = _rope_tables(seq)
    gains = norm_gains.reshape(norm_gains.shape[0], 4, 1, d)
    xs = x.reshape(batch * seq, d)

    qkv0 = pool_width
    proj = _inproj(xs, gains[0, 0], w_in_ab[0].astype(BF16),
                   _colscale(w_in_ab.shape[2], [(qkv0, qkv0 + n_groups * att)]), cos, sin, seq=seq,
                   rope_cols=(qkv0, qkv0 + 2 * n_groups * att))
    a_out = _pool(proj, pool_w[0].astype(BF16), pool_scale[0].reshape(1, pool_width), seq=seq)
    b_out = _dilated(proj, batch=batch, seq=seq, col0=qkv0)
    xs = _outproj(a_out, b_out, w_out_ab[0].astype(BF16), xs, gains[0, 1])
    ffn_w = (ffn_gate.astype(BF16), ffn_up.astype(BF16), ffn_down.astype(BF16))
    xs = _ffn(xs, gains[0, 2], gains[0, 3], *ffn_w, 0)

    moba0 = 3 * att
    proj = _inproj(xs, gains[1, 0], w_in_cd[0].astype(BF16),
                   _colscale(w_in_cd.shape[2], [(0, att), (moba0, moba0 + att)]), cos, sin, seq=seq,
                   rope_cols=(moba0, moba0 + 2 * att))
    c_out = _stick_breaking(proj, batch=batch, seq=seq, col0=0)
    d_out = _moba(proj, batch=batch, seq=seq, col0=moba0)
    xs = _outproj(c_out, d_out, w_out_cd[0].astype(BF16), xs, gains[1, 1])
    xs = _ffn(xs, gains[1, 2], gains[1, 3], *ffn_w, 1)
    return xs.reshape(batch, seq, d)
```

```python
import functools

import jax
import jax.numpy as jnp
from jax import lax
from jax.experimental import pallas as pl
from jax.experimental.pallas import tpu as pltpu

F32 = jnp.float32
BF16 = jnp.bfloat16

HEAD_DIM = 128
EPS = 1e-6
ROPE_THETA = 10000.0
POOL_WINDOWS = (2, 4, 8, 16)
POOL_GROUP = 256
POOL_HALO = 16
DIL_PATTERNS = ((128, 1), (512, 4), (2048, 16))
DIL_QBLOCK = 128
DIL_UNROLL = 8
ATT_HEADS = 8
MOBA_BLOCK = 256
MOBA_TOPK = 3
SB_BLOCK = 256
SB_TILE = 2 * SB_BLOCK
SB_LINEAR_ABOVE = 64.0
LOG2E = 1.4426950408889634
Q_SCALE = LOG2E * HEAD_DIM ** -0.5
MASK_VALUE = -1e30
MOBA_UNSELECTED = -30000.0

V7X_VMEM_BYTES = 64 * 1024 * 1024
VMEM_LIMIT = V7X_VMEM_BYTES - 8 * 1024 * 1024


def _params(semantics):
    return pltpu.CompilerParams(dimension_semantics=semantics, vmem_limit_bytes=VMEM_LIMIT)


def _dot(a, b):
    return jnp.dot(a, b, preferred_element_type=F32)


def _dot_nt(a, b):
    return lax.dot_general(a, b, (((1,), (1,)), ((), ())), preferred_element_type=F32)


def _rms_norm(x, gain):
    return x * lax.rsqrt(jnp.mean(x * x, axis=-1, keepdims=True) + EPS) * gain


def _row_blocks(rows, at_most):
    return max(n for n in range(1, at_most + 1) if rows % (16 * n) == 0)


def _call_with_casts(kernel, casts, *, grid, in_specs, out_specs, out_shape, **kwargs):
    steps = 1
    for g in grid:
        steps *= g

    def flat(idx):
        s = idx[0]
        for g, i in zip(grid[1:], idx[1:]):
            s = s * g + i
        return s

    jobs = []
    cast_in, cast_out, cast_shapes = [], [], []
    for w, layer in casts:
        _, rows, cols = w.shape
        count = _row_blocks(rows, max(steps // len(casts), 1))
        first = sum(c for _, c in jobs)
        jobs.append((first, count))

        def block(*idx, first=first, count=count):
            return jnp.clip(flat(idx) - first, 0, count - 1)

        cast_in.append(pl.BlockSpec((None, rows // count, cols), lambda *idx, block=block, layer=layer:
                                    (layer, block(*idx), 0)))
        cast_out.append(pl.BlockSpec((rows // count, cols), lambda *idx, block=block: (block(*idx), 0)))
        cast_shapes.append(jax.ShapeDtypeStruct((rows, cols), BF16))

    n_in, n_cast = len(in_specs), len(casts)

    def with_casts(*refs):
        main_in, src = refs[:n_in], refs[n_in:n_in + n_cast]
        main_out, dst = refs[n_in + n_cast], refs[n_in + n_cast + 1:n_in + 2 * n_cast + 1]
        kernel(*main_in, main_out, *refs[n_in + 2 * n_cast + 1:])
        step = flat([pl.program_id(a) for a in range(len(grid))])
        for s, d, (first, count) in zip(src, dst, jobs):
            @pl.when(jnp.logical_and(step >= first, step < first + count))
            def _(s=s, d=d):
                d[...] = s[...].astype(BF16)

    call = pl.pallas_call(
        with_casts, grid=grid, in_specs=list(in_specs) + cast_in, out_specs=[out_specs] + cast_out,
        out_shape=[out_shape] + cast_shapes, **kwargs)

    def run(*args):
        res = call(*args, *[w for w, _ in casts])
        return res[0], list(res[1:])

    return run


def _inproj_kernel(x_ref, g_ref, w_ref, cs_ref, cos_ref, sin_ref, o_ref, hn_ref, *, rope_lo, rope_hi):
    j = pl.program_id(1)

    @pl.when(j == 0)
    def _():
        hn_ref[...] = _rms_norm(x_ref[...], g_ref[...]).astype(BF16)

    acc = _dot(hn_ref[...], w_ref[...]) * cs_ref[...]
    is_rope = jnp.logical_and(j >= rope_lo, j < rope_hi)
    cos = jnp.where(is_rope, cos_ref[...], 1.0)
    sin = jnp.where(is_rope, sin_ref[...], 0.0)
    for c in range(acc.shape[1] // HEAD_DIM):
        blk = acc[:, c * HEAD_DIM:(c + 1) * HEAD_DIM]
        rot = pltpu.roll(blk, HEAD_DIM // 2, 1)
        o_ref[:, c * HEAD_DIM:(c + 1) * HEAD_DIM] = (blk * cos + rot * sin).astype(o_ref.dtype)


def _inproj(x, gain, w, colscale, cos, sin, *, seq, rope_cols, tm=1024, tn=1024):
    t, d = x.shape
    n = w.shape[1]
    assert t % tm == 0 and n % tn == 0 and seq % tm == 0
    assert rope_cols[0] % tn == 0 and rope_cols[1] % tn == 0
    pos_blocks = seq // tm
    kern = functools.partial(_inproj_kernel, rope_lo=rope_cols[0] // tn, rope_hi=rope_cols[1] // tn)
    return pl.pallas_call(
        kern,
        grid=(t // tm, n // tn),
        in_specs=[
            pl.BlockSpec((tm, d), lambda i, j: (i, 0)),
            pl.BlockSpec((1, d), lambda i, j: (0, 0)),
            pl.BlockSpec((d, tn), lambda i, j: (0, j)),
            pl.BlockSpec((1, tn), lambda i, j: (0, j)),
            pl.BlockSpec((tm, HEAD_DIM), lambda i, j: (i % pos_blocks, 0)),
            pl.BlockSpec((tm, HEAD_DIM), lambda i, j: (i % pos_blocks, 0)),
        ],
        out_specs=pl.BlockSpec((tm, tn), lambda i, j: (i, j)),
        out_shape=jax.ShapeDtypeStruct((t, n), BF16),
        scratch_shapes=[pltpu.VMEM((tm, d), BF16)],
        compiler_params=_params(("parallel", "arbitrary")),
        name="inproj",
    )(x, gain, w, colscale, cos, sin)


def _pool_kernel(u_ref, halo_ref, w_ref, sc_ref, o_ref, ext_ref, *, tm, seq):
    pos0 = (pl.program_id(0) * tm) % seq
    ext_ref[pl.ds(POOL_HALO, tm), :] = u_ref[...].astype(F32)
    ext_ref[pl.ds(0, POOL_HALO), :] = jnp.where(pos0 == 0, 0.0, halo_ref[...].astype(F32))
    pos = pos0 + lax.broadcasted_iota(jnp.int32, (tm, 1), 0)
    for g, w in enumerate(POOL_WINDOWS):
        cols = pl.ds(g * POOL_GROUP, POOL_GROUP)
        cur = ext_ref[pl.ds(POOL_HALO, tm), cols]
        tot = cur
        for back in range(1, w):
            tot = tot + ext_ref[pl.ds(POOL_HALO - back, tm), cols]
        cnt = jnp.minimum(pos + 1, w).astype(F32)
        pooled = tot / cnt - cur
        y = _dot(pooled.astype(BF16), w_ref[g])
        o_ref[:, g * POOL_GROUP:(g + 1) * POOL_GROUP] = (
            y * sc_ref[:, g * POOL_GROUP:(g + 1) * POOL_GROUP]).astype(o_ref.dtype)


def _pool(proj, w, scale, *, seq, tm=512):
    t = proj.shape[0]
    width = len(POOL_WINDOWS) * POOL_GROUP
    assert seq % tm == 0 and tm % POOL_HALO == 0
    halo_blocks = tm // POOL_HALO
    kern = functools.partial(_pool_kernel, tm=tm, seq=seq)
    return pl.pallas_call(
        kern,
        grid=(t // tm,),
        in_specs=[
            pl.BlockSpec((tm, width), lambda i: (i, 0)),
            pl.BlockSpec((POOL_HALO, width), lambda i: (jnp.maximum(i * halo_blocks - 1, 0), 0)),
            pl.BlockSpec(w.shape, lambda i: (0, 0, 0)),
            pl.BlockSpec((1, width), lambda i: (0, 0)),
        ],
        out_specs=pl.BlockSpec((tm, width), lambda i: (i, 0)),
        out_shape=jax.ShapeDtypeStruct((t, width), BF16),
        scratch_shapes=[pltpu.VMEM((tm + POOL_HALO, width), F32)],
        compiler_params=_params(("parallel",)),
        name="pool",
    )(proj, proj, w, scale)


def _dil_kernel(q_ref, k_ref, v_ref, o_ref, qd_ref, kd_ref, vd_ref, qw_ref, kw_ref, vw_ref,
                m_ref, l_ref, acc_ref, *, seq):
    g = pl.program_id(2)
    wide_refs = (qw_ref, kw_ref, vw_ref)
    qb = DIL_QBLOCK
    nblocks = seq // qb

    @pl.when(g == 0)
    def _():
        m_ref[...] = jnp.full(m_ref.shape, MASK_VALUE, F32)
        l_ref[...] = jnp.zeros(l_ref.shape, F32)
        acc_ref[...] = jnp.zeros(acc_ref.shape, F32)
        kd_ref[pl.ds(0, qb), :] = jnp.zeros((qb, HEAD_DIM), BF16)
        vd_ref[pl.ds(0, qb), :] = jnp.zeros((qb, HEAD_DIM), BF16)

    row = lax.broadcasted_iota(jnp.int32, (qb, 2 * qb), 0)
    col = lax.broadcasted_iota(jnp.int32, (qb, 2 * qb), 1)
    band = jnp.logical_and(col >= row, col <= row + qb)
    own_block = col >= qb

    def run_group(dil):
        nblk = seq // dil // qb
        assert nblk & (nblk - 1) == 0
        shift = nblk.bit_length() - 1

        def positions(idx):
            r = lax.shift_right_logical(idx, shift)
            n = jnp.bitwise_and(idx, nblk - 1)
            start = r + dil * qb * n
            return (pl.ds(start, qb, stride=dil) if dil > 1 else pl.ds(pl.multiple_of(start, qb), qb)), n

        if dil > 1:
            def widen(idx, carry):
                rows = pl.ds(pl.multiple_of(idx * qb, qb), qb)
                for src_ref, wide_ref in zip((q_ref, k_ref, v_ref), wide_refs):
                    wide_ref[rows, :] = src_ref[rows, :].astype(F32)
                return carry

            lax.fori_loop(0, nblocks, widen, 0, unroll=4)
            sources = wide_refs
        else:
            sources = (q_ref, k_ref, v_ref)

        def gather(idx, carry):
            src, _ = positions(idx)
            qd_ref[pl.ds(pl.multiple_of(idx * qb, qb), qb), :] = sources[0][src, :].astype(BF16)
            dst = pl.ds(pl.multiple_of((idx + 1) * qb, qb), qb)
            kd_ref[dst, :] = sources[1][src, :].astype(BF16)
            vd_ref[dst, :] = sources[2][src, :].astype(BF16)
            return carry

        lax.fori_loop(0, nblocks, gather, 0, unroll=4)

        def window(idx):
            return pl.ds(pl.multiple_of(idx * qb, qb), 2 * qb)

        def block_scores(idx):
            q = qd_ref[pl.ds(pl.multiple_of(idx * qb, qb), qb), :]
            return _dot_nt(q, kd_ref[window(idx), :])

        def block_update(idx, s):
            cur, n = positions(idx)
            s = jnp.where(jnp.logical_and(band, jnp.logical_or(own_block, n > 0)), s, MASK_VALUE)
            m_old = m_ref[cur, :]
            m_new = jnp.maximum(m_old, jnp.max(s, axis=-1, keepdims=True))
            p = jnp.exp2(s - jnp.concatenate([m_new, m_new], axis=1))
            alpha = jnp.exp2(m_old - m_new)
            l_ref[cur, :] = alpha * l_ref[cur, :] + jnp.sum(p, axis=-1, keepdims=True)
            acc_ref[cur, :] = alpha * acc_ref[cur, :] + _dot(p.astype(BF16), vd_ref[window(idx), :])
            m_ref[cur, :] = m_new

        def body(it, carry):
            first = it * DIL_UNROLL
            scores = [block_scores(first + u) for u in range(DIL_UNROLL)]
            for u in range(DIL_UNROLL):
                block_update(first + u, scores[u])
            return carry

        lax.fori_loop(0, nblocks // DIL_UNROLL, body, 0)

    for gi, (_, dil) in enumerate(DIL_PATTERNS):
        pl.when(g == gi)(functools.partial(run_group, dil))

    @pl.when(g == len(DIL_PATTERNS) - 1)
    def _():
        o_ref[...] = (acc_ref[...] / l_ref[...]).astype(o_ref.dtype)


def _dilated(proj, casts, *, batch, seq, col0):
    n_groups = len(DIL_PATTERNS)
    per = n_groups * ATT_HEADS
    c0 = col0 // HEAD_DIM
    kern = functools.partial(_dil_kernel, seq=seq)

    def spec(which):
        return pl.BlockSpec((seq, HEAD_DIM), lambda b, h, g: (b, c0 + which * per + g * ATT_HEADS + h))

    return _call_with_casts(
        kern, casts,
        grid=(batch, ATT_HEADS, n_groups),
        in_specs=[spec(0), spec(1), spec(2)],
        out_specs=pl.BlockSpec((seq, HEAD_DIM), lambda b, h, g: (b, h)),
        out_shape=jax.ShapeDtypeStruct((batch * seq, ATT_HEADS * HEAD_DIM), BF16),
        scratch_shapes=[
            pltpu.VMEM((seq, HEAD_DIM), BF16),
            pltpu.VMEM((seq + DIL_QBLOCK, HEAD_DIM), BF16),
            pltpu.VMEM((seq + DIL_QBLOCK, HEAD_DIM), BF16),
        ] + [pltpu.VMEM((seq, HEAD_DIM), F32)] * 6,
        compiler_params=_params(("arbitrary", "arbitrary", "arbitrary")),
        name="dilated",
    )(proj, proj, proj)


def _outproj_kernel(a_ref, b_ref, wa_ref, wb_ref, x_ref, g_ref, o_ref):
    y = _dot(a_ref[...], wa_ref[...]) + _dot(b_ref[...], wb_ref[...])
    o_ref[...] = x_ref[...] + _rms_norm(y, g_ref[...])


def _outproj(a, b, w, x, gain, *, tm=512):
    t, d = x.shape
    half = a.shape[1]
    assert w.shape == (2 * half, d) and t % tm == 0
    return pl.pallas_call(
        _outproj_kernel,
        grid=(t // tm,),
        in_specs=[
            pl.BlockSpec((tm, half), lambda i: (i, 0)),
            pl.BlockSpec((tm, half), lambda i: (i, 0)),
            pl.BlockSpec((half, d), lambda i: (0, 0)),
            pl.BlockSpec((half, d), lambda i: (1, 0)),
            pl.BlockSpec((tm, d), lambda i: (i, 0)),
            pl.BlockSpec((1, d), lambda i: (0, 0)),
        ],
        out_specs=pl.BlockSpec((tm, d), lambda i: (i, 0)),
        out_shape=jax.ShapeDtypeStruct((t, d), F32),
        compiler_params=_params(("parallel",)),
        name="outproj",
    )(a, b, w, w, x, gain)


def _ffn_kernel(x_ref, g_in_ref, g_out_ref, wg_ref, wu_ref, wd_ref, o_ref, hn_ref, acc_ref):
    j = pl.program_id(1)

    @pl.when(j == 0)
    def _():
        hn_ref[...] = _rms_norm(x_ref[...], g_in_ref[...]).astype(BF16)
        acc_ref[...] = jnp.zeros(acc_ref.shape, F32)

    h = hn_ref[...]
    gate = _dot(h, wg_ref[...])
    up = _dot(h, wu_ref[...])
    act = (gate * jax.nn.sigmoid(gate)) * up
    acc_ref[...] += _dot(act.astype(BF16), wd_ref[...])

    @pl.when(j == pl.num_programs(1) - 1)
    def _():
        o_ref[...] = x_ref[...] + _rms_norm(acc_ref[...], g_out_ref[...])


def _ffn(x, g_in, g_out, wg, wu, wd, *, tm=512, tf=512):
    t, d = x.shape
    f = wg.shape[1]
    assert t % tm == 0 and f % tf == 0
    return pl.pallas_call(
        _ffn_kernel,
        grid=(t // tm, f // tf),
        in_specs=[
            pl.BlockSpec((tm, d), lambda i, j: (i, 0)),
            pl.BlockSpec((1, d), lambda i, j: (0, 0)),
            pl.BlockSpec((1, d), lambda i, j: (0, 0)),
            pl.BlockSpec((d, tf), lambda i, j: (0, j)),
            pl.BlockSpec((d, tf), lambda i, j: (0, j)),
            pl.BlockSpec((tf, d), lambda i, j: (j, 0)),
        ],
        out_specs=pl.BlockSpec((tm, d), lambda i, j: (i, 0)),
        out_shape=jax.ShapeDtypeStruct((t, d), F32),
        scratch_shapes=[pltpu.VMEM((tm, d), BF16), pltpu.VMEM((tm, d), F32)],
        compiler_params=_params(("parallel", "arbitrary")),
        name="ffn",
    )(x, g_in, g_out, wg, wu, wd)


def _sb_suffix(z, tri, mask):
    tb = SB_BLOCK
    soft = jnp.where(z > SB_LINEAR_ABOVE, z, jnp.log(1.0 + jnp.exp2(z)) * LOG2E)
    if mask is not None:
        soft = jnp.where(mask, soft, 0.0)
    p16 = soft.astype(BF16)
    left, right = slice(0, tb), slice(tb, 2 * tb)
    suffix_l = _dot(p16[:, left], tri)
    suffix_r = _dot(p16[:, right], tri)
    total_l = jnp.sum(soft[:, left], axis=-1, keepdims=True)
    total_r = jnp.sum(soft[:, right], axis=-1, keepdims=True)
    return suffix_l, suffix_r, total_l, total_r


def _sb_weights(z, sums, carry, mask):
    tb = SB_BLOCK
    suffix_l, suffix_r, total_l, total_r = sums
    expo = jnp.concatenate([z[:, :tb] - suffix_l - (carry + total_r), z[:, tb:] - suffix_r - carry], axis=1)
    a = jnp.exp2(expo)
    if mask is not None:
        a = jnp.where(mask, a, 0.0)
    return a.astype(BF16), carry + total_r + total_l


def _sb_kernel(q_ref, k_ref, v_ref, tri_ref, o_ref, *, heads):
    qi = pl.program_id(2)
    tq = SB_TILE
    tri = tri_ref[...]
    row = lax.broadcasted_iota(jnp.int32, (tq, tq), 0)
    col = lax.broadcasted_iota(jnp.int32, (tq, tq), 1)
    strictly_past = col < row

    def head_cols(h):
        return slice(h * HEAD_DIM, (h + 1) * HEAD_DIM)

    qs = [q_ref[:, head_cols(h)] for h in range(heads)]

    def tiles(rows, accs, carries, mask):
        zs = [_dot_nt(qs[h], k_ref[rows, head_cols(h)]) for h in range(heads)]
        sums = [_sb_suffix(zs[h], tri, mask) for h in range(heads)]
        out = []
        for h in range(heads):
            a, carry = _sb_weights(zs[h], sums[h], carries[h], mask)
            out += [accs[h] + _dot(a, v_ref[rows, head_cols(h)]), carry]
        return out

    diag = pl.ds(pl.multiple_of(qi * tq, tq), tq)
    state = tiles(diag, [jnp.zeros((tq, HEAD_DIM), F32)] * heads, [jnp.zeros((tq, 1), F32)] * heads, strictly_past)

    def body(it, st):
        rows = pl.ds(pl.multiple_of((qi - 1 - it) * tq, tq), tq)
        return tuple(tiles(rows, st[0::2], st[1::2], None))

    state = lax.fori_loop(0, qi, body, tuple(state))
    for h in range(heads):
        o_ref[:, head_cols(h)] = state[2 * h].astype(o_ref.dtype)


def _stick_breaking(proj, casts, *, batch, seq, col0, heads_per_step=2):
    tb, tq = SB_BLOCK, SB_TILE
    assert tq == 2 * tb and seq % tq == 0
    width = heads_per_step * HEAD_DIM
    c0 = col0 // width
    per = ATT_HEADS // heads_per_step
    nq = seq // tq
    tri = (lax.broadcasted_iota(jnp.int32, (tb, tb), 0) >= lax.broadcasted_iota(jnp.int32, (tb, tb), 1)).astype(BF16)
    kern = functools.partial(_sb_kernel, heads=heads_per_step)
    return _call_with_casts(
        kern, casts,
        grid=(batch, per, nq),
        in_specs=[
            pl.BlockSpec((tq, width), lambda b, h, i: (b * nq + i, c0 + h)),
            pl.BlockSpec((seq, width), lambda b, h, i: (b, c0 + per + h)),
            pl.BlockSpec((seq, width), lambda b, h, i: (b, c0 + 2 * per + h)),
            pl.BlockSpec((tb, tb), lambda b, h, i: (0, 0)),
        ],
        out_specs=pl.BlockSpec((tq, width), lambda b, h, i: (b * nq + i, h)),
        out_shape=jax.ShapeDtypeStruct((batch * seq, ATT_HEADS * HEAD_DIM), BF16),
        compiler_params=_params(("arbitrary", "arbitrary", "arbitrary")),
        name="stick_breaking",
    )(proj, proj, proj, tri)


def _moba_setup(k, kaug_ref, kmean_ref, *, seq):
    blk = MOBA_BLOCK
    nb = kmean_ref.shape[1]
    lane = lax.broadcasted_iota(jnp.int32, (blk, HEAD_DIM), 1)
    for n in range(seq // blk):
        kaug_ref[n * blk:(n + 1) * blk, :HEAD_DIM] = k[n * blk:(n + 1) * blk, :]
        kaug_ref[n * blk:(n + 1) * blk, HEAD_DIM:] = jnp.where(lane == n, 1.0, 0.0).astype(BF16)
    key = lax.broadcasted_iota(jnp.int32, (nb, seq), 1)
    first = lax.broadcasted_iota(jnp.int32, (nb, seq), 0) * blk
    in_block = jnp.logical_and(key >= first, key < first + blk)
    mean = _dot(jnp.where(in_block, 1.0 / blk, 0.0).astype(BF16), k)
    hi = mean.astype(BF16)
    mid = (mean - hi.astype(F32)).astype(BF16)
    lo = (mean - hi.astype(F32) - mid.astype(F32)).astype(BF16)
    kmean_ref[0] = hi
    kmean_ref[1] = mid
    kmean_ref[2] = lo


def _moba_select(q, qi, kmean_ref):
    blk = MOBA_BLOCK
    nb = kmean_ref.shape[1]
    gate = _dot_nt(kmean_ref[0], q) + _dot_nt(kmean_ref[1], q) + _dot_nt(kmean_ref[2], q)
    blk_id = lax.broadcasted_iota(jnp.int32, (nb, blk), 0)
    past = blk_id < qi
    gate = jnp.where(past, gate, -jnp.inf)
    beaten = jnp.zeros((nb, blk), F32)
    for m in range(nb):
        gm = gate[m:m + 1, :]
        ahead = jnp.logical_or(gm > gate, jnp.logical_and(gm == gate, blk_id > m))
        beaten = beaten + jnp.where(ahead, 1.0, 0.0)
    keep = jnp.logical_or(jnp.logical_and(past, beaten < MOBA_TOPK), blk_id == qi)
    bias = jnp.where(keep, 0.0, MOBA_UNSELECTED)
    bias = jnp.concatenate([bias, jnp.zeros((HEAD_DIM - nb, blk), F32)], axis=0)
    return jnp.concatenate([q, bias.T.astype(BF16)], axis=1)


def _moba_kernel(q_ref, k_ref, v_ref, o_ref, kaug_ref, kmean_ref, *, seq, heads):
    qi = pl.program_id(2)
    blk = MOBA_BLOCK
    pair = 2 * blk

    def head_cols(h):
        return slice(h * HEAD_DIM, (h + 1) * HEAD_DIM)

    @pl.when(qi == 0)
    def _():
        for h in range(heads):
            _moba_setup(k_ref[:, head_cols(h)], kaug_ref.at[h], kmean_ref.at[h], seq=seq)

    qa = [_moba_select(q_ref[:, head_cols(h)], qi, kmean_ref.at[h]) for h in range(heads)]

    def scores(h, j):
        rows = pl.ds(pl.multiple_of(j * pair, pair), pair)
        return _dot_nt(qa[h], kaug_ref[h, rows, :]), v_ref[rows, head_cols(h)]

    last = lax.shift_right_logical(qi, 1)
    row = lax.broadcasted_iota(jnp.int32, (blk, pair), 0)
    col = lax.broadcasted_iota(jnp.int32, (blk, pair), 1)
    causal = col <= row + (qi - 2 * last) * blk
    state = []
    for h in range(heads):
        s, v = scores(h, last)
        s = jnp.where(causal, s, MASK_VALUE)
        m = jnp.max(s, axis=-1, keepdims=True)
        p = jnp.exp2(s - m)
        state += [m, jnp.sum(p, axis=-1, keepdims=True), _dot(p.astype(BF16), v)]

    def body(j, st):
        out = []
        nxt = scores(0, j)
        for h in range(heads):
            s, v = nxt
            if h + 1 < heads:
                nxt = scores(h + 1, j)
            m_old, l_old, acc = st[3 * h:3 * h + 3]
            m_new = jnp.maximum(m_old, jnp.max(s, axis=-1, keepdims=True))
            alpha = jnp.exp2(m_old - m_new)
            p = jnp.exp2(s - m_new)
            out += [m_new, alpha * l_old + jnp.sum(p, axis=-1, keepdims=True),
                    alpha * acc + _dot(p.astype(BF16), v)]
        return tuple(out)

    state = lax.fori_loop(0, last, body, tuple(state))
    for h in range(heads):
        o_ref[:, head_cols(h)] = (state[3 * h + 2] / state[3 * h + 1]).astype(o_ref.dtype)


def _moba(proj, *, batch, seq, col0, heads_per_step=4):
    blk = MOBA_BLOCK
    nq = seq // blk
    assert nq % 2 == 0
    nb_pad = -(-nq // 16) * 16
    assert nb_pad <= HEAD_DIM
    width = heads_per_step * HEAD_DIM
    c0 = col0 // width
    per = ATT_HEADS // heads_per_step
    kern = functools.partial(_moba_kernel, seq=seq, heads=heads_per_step)
    return pl.pallas_call(
        kern,
        grid=(batch, per, nq),
        in_specs=[
            pl.BlockSpec((blk, width), lambda b, h, i: (b * nq + i, c0 + h)),
            pl.BlockSpec((seq, width), lambda b, h, i: (b, c0 + per + h)),
            pl.BlockSpec((seq, width), lambda b, h, i: (b, c0 + 2 * per + h)),
        ],
        out_specs=pl.BlockSpec((blk, width), lambda b, h, i: (b * nq + i, h)),
        out_shape=jax.ShapeDtypeStruct((batch * seq, ATT_HEADS * HEAD_DIM), BF16),
        scratch_shapes=[
            pltpu.VMEM((heads_per_step, seq, 2 * HEAD_DIM), BF16),
            pltpu.VMEM((heads_per_step, 3, nb_pad, HEAD_DIM), BF16),
        ],
        compiler_params=_params(("parallel", "parallel", "arbitrary")),
        name="moba",
    )(proj, proj, proj)


def _rope_tables(seq):
    inv = 1.0 / (ROPE_THETA ** (jnp.arange(0, HEAD_DIM, 2, dtype=F32) / HEAD_DIM))
    ang = jnp.arange(seq, dtype=F32)[:, None] * inv[None, :]
    cos, sin = jnp.cos(ang), jnp.sin(ang)
    return jnp.concatenate([cos, cos], axis=-1), jnp.concatenate([-sin, sin], axis=-1)


def _colscale(n, q_ranges):
    cs = jnp.ones((1, n), F32)
    for lo, hi in q_ranges:
        cs = cs.at[:, lo:hi].set(Q_SCALE)
    return cs


def kernel(x, norm_gains, w_in_ab, pool_w, pool_scale, w_out_ab, w_in_cd, w_out_cd, ffn_gate, ffn_up, ffn_down):
    batch, seq, d = x.shape
    att = ATT_HEADS * HEAD_DIM
    pool_width = len(POOL_WINDOWS) * POOL_GROUP
    n_groups = len(DIL_PATTERNS)
    cos, sin = _rope_tables(seq)
    gains = norm_gains.reshape(norm_gains.shape[0], 4, 1, d)
    xs = x.reshape(batch * seq, d)

    qkv0 = pool_width
    proj = _inproj(xs, gains[0, 0], w_in_ab[0].astype(BF16),
                   _colscale(w_in_ab.shape[2], [(qkv0, qkv0 + n_groups * att)]), cos, sin, seq=seq,
                   rope_cols=(qkv0, qkv0 + 2 * n_groups * att))
    a_out = _pool(proj, pool_w[0].astype(BF16), pool_scale[0].reshape(1, pool_width), seq=seq)
    ffn_weights = (ffn_gate, ffn_up, ffn_down)
    b_out, ffn_w = _dilated(proj, [(w, 0) for w in ffn_weights], batch=batch, seq=seq, col0=qkv0)
    xs = _outproj(a_out, b_out, w_out_ab[0].astype(BF16), xs, gains[0, 1])
    xs = _ffn(xs, gains[0, 2], gains[0, 3], *ffn_w)

    moba0 = 3 * att
    proj = _inproj(xs, gains[1, 0], w_in_cd[0].astype(BF16),
                   _colscale(w_in_cd.shape[2], [(0, att), (moba0, moba0 + att)]), cos, sin, seq=seq,
                   rope_cols=(moba0, moba0 + 2 * att))
    c_out, ffn_w = _stick_breaking(proj, [(w, 1) for w in ffn_weights], batch=batch, seq=seq, col0=0)
    d_out = _moba(proj, batch=batch, seq=seq, col0=moba0)
    xs = _outproj(c_out, d_out, w_out_cd[0].astype(BF16), xs, gains[1, 1])
    xs = _ffn(xs, gains[1, 2], gains[1, 3], *ffn_w)
    return xs.reshape(batch, seq, d)
```

```python
import functools

import jax
import jax.numpy as jnp
from jax import lax
from jax.experimental import pallas as pl
from jax.experimental.pallas import tpu as pltpu

F32 = jnp.float32
BF16 = jnp.bfloat16

HEAD_DIM = 128
EPS = 1e-6
ROPE_THETA = 10000.0
POOL_WINDOWS = (2, 4, 8, 16)
POOL_GROUP = 256
POOL_HALO = 16
DIL_PATTERNS = ((128, 1), (512, 4), (2048, 16))
DIL_QBLOCK = 128
DIL_UNROLL = 8
ATT_HEADS = 8
MOBA_BLOCK = 256
MOBA_TOPK = 3
SB_BLOCK = 256
SB_TILE = 2 * SB_BLOCK
SB_LINEAR_ABOVE = 64.0
LOG2E = 1.4426950408889634
Q_SCALE = LOG2E * HEAD_DIM ** -0.5
MASK_VALUE = -1e30
MOBA_UNSELECTED = -30000.0

V7X_VMEM_BYTES = 64 * 1024 * 1024
VMEM_LIMIT = V7X_VMEM_BYTES - 8 * 1024 * 1024


def _params(semantics):
    return pltpu.CompilerParams(dimension_semantics=semantics, vmem_limit_bytes=VMEM_LIMIT)


def _dot(a, b):
    return jnp.dot(a, b, preferred_element_type=F32)


def _dot_nt(a, b):
    return lax.dot_general(a, b, (((1,), (1,)), ((), ())), preferred_element_type=F32)


def _rms_norm(x, gain):
    return x * lax.rsqrt(jnp.mean(x * x, axis=-1, keepdims=True) + EPS) * gain


def _row_blocks(rows, at_most):
    return max(n for n in range(1, at_most + 1) if rows % (16 * n) == 0)


def _call_with_casts(kernel, casts, *, grid, in_specs, out_specs, out_shape, together=False, **kwargs):
    steps = 1
    for g in grid:
        steps *= g
    share = 1 if together else max(len(casts), 1)

    def flat(idx):
        s = idx[0]
        for g, i in zip(grid[1:], idx[1:]):
            s = s * g + i
        return s

    jobs = []
    cast_in, cast_out, cast_shapes = [], [], []
    for w, layer in casts:
        _, rows, cols = w.shape
        count = _row_blocks(rows, max(steps // share, 1))
        first = 0 if together else sum(c for _, c in jobs)
        jobs.append((first, count))

        def block(*idx, first=first, count=count):
            return jnp.clip(flat(idx) - first, 0, count - 1)

        cast_in.append(pl.BlockSpec((None, rows // count, cols), lambda *idx, block=block, layer=layer:
                                    (layer, block(*idx), 0)))
        cast_out.append(pl.BlockSpec((rows // count, cols), lambda *idx, block=block: (block(*idx), 0)))
        cast_shapes.append(jax.ShapeDtypeStruct((rows, cols), BF16))

    n_in, n_cast = len(in_specs), len(casts)

    def with_casts(*refs):
        main_in, src = refs[:n_in], refs[n_in:n_in + n_cast]
        main_out, dst = refs[n_in + n_cast], refs[n_in + n_cast + 1:n_in + 2 * n_cast + 1]
        kernel(*main_in, main_out, *refs[n_in + 2 * n_cast + 1:])
        step = flat([pl.program_id(a) for a in range(len(grid))])
        for s, d, (first, count) in zip(src, dst, jobs):
            @pl.when(jnp.logical_and(step >= first, step < first + count))
            def _(s=s, d=d):
                d[...] = s[...].astype(BF16)

    call = pl.pallas_call(
        with_casts, grid=grid, in_specs=list(in_specs) + cast_in, out_specs=[out_specs] + cast_out,
        out_shape=[out_shape] + cast_shapes, **kwargs)

    def run(*args):
        res = call(*args, *[w for w, _ in casts])
        return res[0], list(res[1:])

    return run


def _inproj_kernel(x_ref, g_ref, w_ref, cs_ref, cos_ref, sin_ref, o_ref, hn_ref, *, rope_lo, rope_hi):
    j = pl.program_id(1)

    @pl.when(j == 0)
    def _():
        hn_ref[...] = _rms_norm(x_ref[...], g_ref[...]).astype(BF16)

    acc = _dot(hn_ref[...], w_ref[...]) * cs_ref[...]
    is_rope = jnp.logical_and(j >= rope_lo, j < rope_hi)
    cos = jnp.where(is_rope, cos_ref[...], 1.0)
    sin = jnp.where(is_rope, sin_ref[...], 0.0)
    for c in range(acc.shape[1] // HEAD_DIM):
        blk = acc[:, c * HEAD_DIM:(c + 1) * HEAD_DIM]
        rot = pltpu.roll(blk, HEAD_DIM // 2, 1)
        o_ref[:, c * HEAD_DIM:(c + 1) * HEAD_DIM] = (blk * cos + rot * sin).astype(o_ref.dtype)


def _inproj(x, gain, w, colscale, cos, sin, *, seq, rope_cols, tm=1024, tn=1024):
    t, d = x.shape
    n = w.shape[1]
    assert t % tm == 0 and n % tn == 0 and seq % tm == 0
    assert rope_cols[0] % tn == 0 and rope_cols[1] % tn == 0
    pos_blocks = seq // tm
    kern = functools.partial(_inproj_kernel, rope_lo=rope_cols[0] // tn, rope_hi=rope_cols[1] // tn)
    return pl.pallas_call(
        kern,
        grid=(t // tm, n // tn),
        in_specs=[
            pl.BlockSpec((tm, d), lambda i, j: (i, 0)),
            pl.BlockSpec((1, d), lambda i, j: (0, 0)),
            pl.BlockSpec((d, tn), lambda i, j: (0, j)),
            pl.BlockSpec((1, tn), lambda i, j: (0, j)),
            pl.BlockSpec((tm, HEAD_DIM), lambda i, j: (i % pos_blocks, 0)),
            pl.BlockSpec((tm, HEAD_DIM), lambda i, j: (i % pos_blocks, 0)),
        ],
        out_specs=pl.BlockSpec((tm, tn), lambda i, j: (i, j)),
        out_shape=jax.ShapeDtypeStruct((t, n), BF16),
        scratch_shapes=[pltpu.VMEM((tm, d), BF16)],
        compiler_params=_params(("parallel", "arbitrary")),
        name="inproj",
    )(x, gain, w, colscale, cos, sin)


def _pool_kernel(u_ref, halo_ref, w_ref, sc_ref, o_ref, ext_ref, *, tm, seq):
    pos0 = (pl.program_id(0) * tm) % seq
    ext_ref[pl.ds(POOL_HALO, tm), :] = u_ref[...].astype(F32)
    ext_ref[pl.ds(0, POOL_HALO), :] = jnp.where(pos0 == 0, 0.0, halo_ref[...].astype(F32))
    pos = pos0 + lax.broadcasted_iota(jnp.int32, (tm, 1), 0)
    for g, w in enumerate(POOL_WINDOWS):
        cols = pl.ds(g * POOL_GROUP, POOL_GROUP)
        cur = ext_ref[pl.ds(POOL_HALO, tm), cols]
        tot = cur
        for back in range(1, w):
            tot = tot + ext_ref[pl.ds(POOL_HALO - back, tm), cols]
        cnt = jnp.minimum(pos + 1, w).astype(F32)
        pooled = tot / cnt - cur
        y = _dot(pooled.astype(BF16), w_ref[g])
        o_ref[:, g * POOL_GROUP:(g + 1) * POOL_GROUP] = (
            y * sc_ref[:, g * POOL_GROUP:(g + 1) * POOL_GROUP]).astype(o_ref.dtype)


def _pool(proj, w, scale, casts, *, seq, tm=512):
    t = proj.shape[0]
    width = len(POOL_WINDOWS) * POOL_GROUP
    assert seq % tm == 0 and tm % POOL_HALO == 0
    halo_blocks = tm // POOL_HALO
    kern = functools.partial(_pool_kernel, tm=tm, seq=seq)
    return _call_with_casts(
        kern, casts, together=True,
        grid=(t // tm,),
        in_specs=[
            pl.BlockSpec((tm, width), lambda i: (i, 0)),
            pl.BlockSpec((POOL_HALO, width), lambda i: (jnp.maximum(i * halo_blocks - 1, 0), 0)),
            pl.BlockSpec(w.shape, lambda i: (0, 0, 0)),
            pl.BlockSpec((1, width), lambda i: (0, 0)),
        ],
        out_specs=pl.BlockSpec((tm, width), lambda i: (i, 0)),
        out_shape=jax.ShapeDtypeStruct((t, width), BF16),
        scratch_shapes=[pltpu.VMEM((tm + POOL_HALO, width), F32)],
        compiler_params=_params(("arbitrary",)),
        name="pool",
    )(proj, proj, w, scale)


def _dil_kernel(q_ref, k_ref, v_ref, o_ref, qd_ref, kd_ref, vd_ref, qw_ref, kw_ref, vw_ref,
                m_ref, l_ref, acc_ref, *, seq):
    g = pl.program_id(2)
    wide_refs = (qw_ref, kw_ref, vw_ref)
    qb = DIL_QBLOCK
    nblocks = seq // qb

    @pl.when(g == 0)
    def _():
        m_ref[...] = jnp.full(m_ref.shape, MASK_VALUE, F32)
        l_ref[...] = jnp.zeros(l_ref.shape, F32)
        acc_ref[...] = jnp.zeros(acc_ref.shape, F32)
        kd_ref[pl.ds(0, qb), :] = jnp.zeros((qb, HEAD_DIM), BF16)
        vd_ref[pl.ds(0, qb), :] = jnp.zeros((qb, HEAD_DIM), BF16)

    row = lax.broadcasted_iota(jnp.int32, (qb, 2 * qb), 0)
    col = lax.broadcasted_iota(jnp.int32, (qb, 2 * qb), 1)
    band = jnp.logical_and(col >= row, col <= row + qb)
    own_block = col >= qb

    def run_group(dil):
        nblk = seq // dil // qb
        assert nblk & (nblk - 1) == 0
        shift = nblk.bit_length() - 1

        def positions(idx):
            r = lax.shift_right_logical(idx, shift)
            n = jnp.bitwise_and(idx, nblk - 1)
            start = r + dil * qb * n
            return (pl.ds(start, qb, stride=dil) if dil > 1 else pl.ds(pl.multiple_of(start, qb), qb)), n

        if dil > 1:
            def widen(idx, carry):
                rows = pl.ds(pl.multiple_of(idx * qb, qb), qb)
                for src_ref, wide_ref in zip((q_ref, k_ref, v_ref), wide_refs):
                    wide_ref[rows, :] = src_ref[rows, :].astype(F32)
                return carry

            lax.fori_loop(0, nblocks, widen, 0, unroll=4)
            sources = wide_refs
        else:
            sources = (q_ref, k_ref, v_ref)

        def gather(idx, carry):
            src, _ = positions(idx)
            qd_ref[pl.ds(pl.multiple_of(idx * qb, qb), qb), :] = sources[0][src, :].astype(BF16)
            dst = pl.ds(pl.multiple_of((idx + 1) * qb, qb), qb)
            kd_ref[dst, :] = sources[1][src, :].astype(BF16)
            vd_ref[dst, :] = sources[2][src, :].astype(BF16)
            return carry

        lax.fori_loop(0, nblocks, gather, 0, unroll=4)

        def window(idx):
            return pl.ds(pl.multiple_of(idx * qb, qb), 2 * qb)

        def block_scores(idx):
            q = qd_ref[pl.ds(pl.multiple_of(idx * qb, qb), qb), :]
            return _dot_nt(q, kd_ref[window(idx), :])

        def block_update(idx, s):
            cur, n = positions(idx)
            s = jnp.where(jnp.logical_and(band, jnp.logical_or(own_block, n > 0)), s, MASK_VALUE)
            m_old = m_ref[cur, :]
            m_new = jnp.maximum(m_old, jnp.max(s, axis=-1, keepdims=True))
            p = jnp.exp2(s - jnp.concatenate([m_new, m_new], axis=1))
            alpha = jnp.exp2(m_old - m_new)
            l_ref[cur, :] = alpha * l_ref[cur, :] + jnp.sum(p, axis=-1, keepdims=True)
            acc_ref[cur, :] = alpha * acc_ref[cur, :] + _dot(p.astype(BF16), vd_ref[window(idx), :])
            m_ref[cur, :] = m_new

        def body(it, carry):
            first = it * DIL_UNROLL
            scores = [block_scores(first + u) for u in range(DIL_UNROLL)]
            for u in range(DIL_UNROLL):
                block_update(first + u, scores[u])
            return carry

        lax.fori_loop(0, nblocks // DIL_UNROLL, body, 0)

    for gi, (_, dil) in enumerate(DIL_PATTERNS):
        pl.when(g == gi)(functools.partial(run_group, dil))

    @pl.when(g == len(DIL_PATTERNS) - 1)
    def _():
        o_ref[...] = (acc_ref[...] / l_ref[...]).astype(o_ref.dtype)


def _dilated(proj, casts, *, batch, seq, col0):
    n_groups = len(DIL_PATTERNS)
    per = n_groups * ATT_HEADS
    c0 = col0 // HEAD_DIM
    kern = functools.partial(_dil_kernel, seq=seq)

    def spec(which):
        return pl.BlockSpec((seq, HEAD_DIM), lambda b, h, g: (b, c0 + which * per + g * ATT_HEADS + h))

    return _call_with_casts(
        kern, casts,
        grid=(batch, ATT_HEADS, n_groups),
        in_specs=[spec(0), spec(1), spec(2)],
        out_specs=pl.BlockSpec((seq, HEAD_DIM), lambda b, h, g: (b, h)),
        out_shape=jax.ShapeDtypeStruct((batch * seq, ATT_HEADS * HEAD_DIM), BF16),
        scratch_shapes=[
            pltpu.VMEM((seq, HEAD_DIM), BF16),
            pltpu.VMEM((seq + DIL_QBLOCK, HEAD_DIM), BF16),
            pltpu.VMEM((seq + DIL_QBLOCK, HEAD_DIM), BF16),
        ] + [pltpu.VMEM((seq, HEAD_DIM), F32)] * 6,
        compiler_params=_params(("arbitrary", "arbitrary", "arbitrary")),
        name="dilated",
    )(proj, proj, proj)


def _outproj_kernel(a_ref, b_ref, wa_ref, wb_ref, x_ref, g_ref, o_ref):
    y = _dot(a_ref[...], wa_ref[...]) + _dot(b_ref[...], wb_ref[...])
    o_ref[...] = x_ref[...] + _rms_norm(y, g_ref[...])


def _outproj(a, b, w, x, gain, *, tm=512):
    t, d = x.shape
    half = a.shape[1]
    assert w.shape == (2 * half, d) and t % tm == 0
    return pl.pallas_call(
        _outproj_kernel,
        grid=(t // tm,),
        in_specs=[
            pl.BlockSpec((tm, half), lambda i: (i, 0)),
            pl.BlockSpec((tm, half), lambda i: (i, 0)),
            pl.BlockSpec((half, d), lambda i: (0, 0)),
            pl.BlockSpec((half, d), lambda i: (1, 0)),
            pl.BlockSpec((tm, d), lambda i: (i, 0)),
            pl.BlockSpec((1, d), lambda i: (0, 0)),
        ],
        out_specs=pl.BlockSpec((tm, d), lambda i: (i, 0)),
        out_shape=jax.ShapeDtypeStruct((t, d), F32),
        compiler_params=_params(("parallel",)),
        name="outproj",
    )(a, b, w, w, x, gain)


def _ffn_kernel(x_ref, g_in_ref, g_out_ref, wg_ref, wu_ref, wd_ref, o_ref, hn_ref, acc_ref):
    j = pl.program_id(1)

    @pl.when(j == 0)
    def _():
        hn_ref[...] = _rms_norm(x_ref[...], g_in_ref[...]).astype(BF16)
        acc_ref[...] = jnp.zeros(acc_ref.shape, F32)

    h = hn_ref[...]
    gate = _dot(h, wg_ref[...])
    up = _dot(h, wu_ref[...])
    act = (gate * jax.nn.sigmoid(gate)) * up
    acc_ref[...] += _dot(act.astype(BF16), wd_ref[...])

    @pl.when(j == pl.num_programs(1) - 1)
    def _():
        o_ref[...] = x_ref[...] + _rms_norm(acc_ref[...], g_out_ref[...])


def _ffn(x, g_in, g_out, wg, wu, wd, *, tm=512, tf=512):
    t, d = x.shape
    f = wg.shape[1]
    assert t % tm == 0 and f % tf == 0
    return pl.pallas_call(
        _ffn_kernel,
        grid=(t // tm, f // tf),
        in_specs=[
            pl.BlockSpec((tm, d), lambda i, j: (i, 0)),
            pl.BlockSpec((1, d), lambda i, j: (0, 0)),
            pl.BlockSpec((1, d), lambda i, j: (0, 0)),
            pl.BlockSpec((d, tf), lambda i, j: (0, j)),
            pl.BlockSpec((d, tf), lambda i, j: (0, j)),
            pl.BlockSpec((tf, d), lambda i, j: (j, 0)),
        ],
        out_specs=pl.BlockSpec((tm, d), lambda i, j: (i, 0)),
        out_shape=jax.ShapeDtypeStruct((t, d), F32),
        scratch_shapes=[pltpu.VMEM((tm, d), BF16), pltpu.VMEM((tm, d), F32)],
        compiler_params=_params(("parallel", "arbitrary")),
        name="ffn",
    )(x, g_in, g_out, wg, wu, wd)


def _sb_suffix(z, tri, mask):
    tb = SB_BLOCK
    soft = jnp.where(z > SB_LINEAR_ABOVE, z, jnp.log(1.0 + jnp.exp2(z)) * LOG2E)
    if mask is not None:
        soft = jnp.where(mask, soft, 0.0)
    p16 = soft.astype(BF16)
    left, right = slice(0, tb), slice(tb, 2 * tb)
    suffix_l = _dot(p16[:, left], tri)
    suffix_r = _dot(p16[:, right], tri)
    total_l = jnp.sum(soft[:, left], axis=-1, keepdims=True)
    total_r = jnp.sum(soft[:, right], axis=-1, keepdims=True)
    return suffix_l, suffix_r, total_l, total_r


def _sb_weights(z, sums, carry, mask):
    tb = SB_BLOCK
    suffix_l, suffix_r, total_l, total_r = sums
    expo = jnp.concatenate([z[:, :tb] - suffix_l - (carry + total_r), z[:, tb:] - suffix_r - carry], axis=1)
    a = jnp.exp2(expo)
    if mask is not None:
        a = jnp.where(mask, a, 0.0)
    return a.astype(BF16), carry + total_r + total_l


def _sb_kernel(q_ref, k_ref, v_ref, tri_ref, o_ref, *, heads):
    qi = pl.program_id(2)
    tq = SB_TILE
    tri = tri_ref[...]
    row = lax.broadcasted_iota(jnp.int32, (tq, tq), 0)
    col = lax.broadcasted_iota(jnp.int32, (tq, tq), 1)
    strictly_past = col < row

    def head_cols(h):
        return slice(h * HEAD_DIM, (h + 1) * HEAD_DIM)

    qs = [q_ref[:, head_cols(h)] for h in range(heads)]

    def tiles(rows, accs, carries, mask):
        zs = [_dot_nt(qs[h], k_ref[rows, head_cols(h)]) for h in range(heads)]
        sums = [_sb_suffix(zs[h], tri, mask) for h in range(heads)]
        out = []
        for h in range(heads):
            a, carry = _sb_weights(zs[h], sums[h], carries[h], mask)
            out += [accs[h] + _dot(a, v_ref[rows, head_cols(h)]), carry]
        return out

    diag = pl.ds(pl.multiple_of(qi * tq, tq), tq)
    state = tiles(diag, [jnp.zeros((tq, HEAD_DIM), F32)] * heads, [jnp.zeros((tq, 1), F32)] * heads, strictly_past)

    def body(it, st):
        rows = pl.ds(pl.multiple_of((qi - 1 - it) * tq, tq), tq)
        return tuple(tiles(rows, st[0::2], st[1::2], None))

    state = lax.fori_loop(0, qi, body, tuple(state))
    for h in range(heads):
        o_ref[:, head_cols(h)] = state[2 * h].astype(o_ref.dtype)


def _stick_breaking(proj, casts, *, batch, seq, col0, heads_per_step=2):
    tb, tq = SB_BLOCK, SB_TILE
    assert tq == 2 * tb and seq % tq == 0
    width = heads_per_step * HEAD_DIM
    c0 = col0 // width
    per = ATT_HEADS // heads_per_step
    nq = seq // tq
    tri = (lax.broadcasted_iota(jnp.int32, (tb, tb), 0) >= lax.broadcasted_iota(jnp.int32, (tb, tb), 1)).astype(BF16)
    kern = functools.partial(_sb_kernel, heads=heads_per_step)
    return _call_with_casts(
        kern, casts,
        grid=(batch, per, nq),
        in_specs=[
            pl.BlockSpec((tq, width), lambda b, h, i: (b * nq + i, c0 + h)),
            pl.BlockSpec((seq, width), lambda b, h, i: (b, c0 + per + h)),
            pl.BlockSpec((seq, width), lambda b, h, i: (b, c0 + 2 * per + h)),
            pl.BlockSpec((tb, tb), lambda b, h, i: (0, 0)),
        ],
        out_specs=pl.BlockSpec((tq, width), lambda b, h, i: (b * nq + i, h)),
        out_shape=jax.ShapeDtypeStruct((batch * seq, ATT_HEADS * HEAD_DIM), BF16),
        compiler_params=_params(("arbitrary", "arbitrary", "arbitrary")),
        name="stick_breaking",
    )(proj, proj, proj, tri)


def _moba_setup(k, kaug_ref, kmean_ref, *, seq):
    blk = MOBA_BLOCK
    nb = kmean_ref.shape[1]
    lane = lax.broadcasted_iota(jnp.int32, (blk, HEAD_DIM), 1)
    for n in range(seq // blk):
        kaug_ref[n * blk:(n + 1) * blk, :HEAD_DIM] = k[n * blk:(n + 1) * blk, :]
        kaug_ref[n * blk:(n + 1) * blk, HEAD_DIM:] = jnp.where(lane == n, 1.0, 0.0).astype(BF16)
    key = lax.broadcasted_iota(jnp.int32, (nb, seq), 1)
    first = lax.broadcasted_iota(jnp.int32, (nb, seq), 0) * blk
    in_block = jnp.logical_and(key >= first, key < first + blk)
    mean = _dot(jnp.where(in_block, 1.0 / blk, 0.0).astype(BF16), k)
    hi = mean.astype(BF16)
    mid = (mean - hi.astype(F32)).astype(BF16)
    lo = (mean - hi.astype(F32) - mid.astype(F32)).astype(BF16)
    kmean_ref[0] = hi
    kmean_ref[1] = mid
    kmean_ref[2] = lo


def _moba_select(q, qi, kmean_ref):
    blk = MOBA_BLOCK
    nb = kmean_ref.shape[1]
    gate = _dot_nt(kmean_ref[0], q) + _dot_nt(kmean_ref[1], q) + _dot_nt(kmean_ref[2], q)
    blk_id = lax.broadcasted_iota(jnp.int32, (nb, blk), 0)
    past = blk_id < qi
    gate = jnp.where(past, gate, -jnp.inf)
    beaten = jnp.zeros((nb, blk), F32)
    for m in range(nb):
        gm = gate[m:m + 1, :]
        ahead = jnp.logical_or(gm > gate, jnp.logical_and(gm == gate, blk_id > m))
        beaten = beaten + jnp.where(ahead, 1.0, 0.0)
    keep = jnp.logical_or(jnp.logical_and(past, beaten < MOBA_TOPK), blk_id == qi)
    bias = jnp.where(keep, 0.0, MOBA_UNSELECTED)
    bias = jnp.concatenate([bias, jnp.zeros((HEAD_DIM - nb, blk), F32)], axis=0)
    return jnp.concatenate([q, bias.T.astype(BF16)], axis=1)


def _moba_kernel(q_ref, k_ref, v_ref, o_ref, kaug_ref, kmean_ref, *, seq, heads):
    qi = pl.program_id(2)
    blk = MOBA_BLOCK
    pair = 2 * blk

    def head_cols(h):
        return slice(h * HEAD_DIM, (h + 1) * HEAD_DIM)

    @pl.when(qi == 0)
    def _():
        for h in range(heads):
            _moba_setup(k_ref[:, head_cols(h)], kaug_ref.at[h], kmean_ref.at[h], seq=seq)

    qa = [_moba_select(q_ref[:, head_cols(h)], qi, kmean_ref.at[h]) for h in range(heads)]

    def scores(h, j):
        rows = pl.ds(pl.multiple_of(j * pair, pair), pair)
        return _dot_nt(qa[h], kaug_ref[h, rows, :]), v_ref[rows, head_cols(h)]

    last = lax.shift_right_logical(qi, 1)
    row = lax.broadcasted_iota(jnp.int32, (blk, pair), 0)
    col = lax.broadcasted_iota(jnp.int32, (blk, pair), 1)
    causal = col <= row + (qi - 2 * last) * blk
    state = []
    for h in range(heads):
        s, v = scores(h, last)
        s = jnp.where(causal, s, MASK_VALUE)
        m = jnp.max(s, axis=-1, keepdims=True)
        p = jnp.exp2(s - m)
        state += [m, jnp.sum(p, axis=-1, keepdims=True), _dot(p.astype(BF16), v)]

    def body(j, st):
        out = []
        nxt = scores(0, j)
        for h in range(heads):
            s, v = nxt
            if h + 1 < heads:
                nxt = scores(h + 1, j)
            m_old, l_old, acc = st[3 * h:3 * h + 3]
            m_new = jnp.maximum(m_old, jnp.max(s, axis=-1, keepdims=True))
            alpha = jnp.exp2(m_old - m_new)
            p = jnp.exp2(s - m_new)
            out += [m_new, alpha * l_old + jnp.sum(p, axis=-1, keepdims=True),
                    alpha * acc + _dot(p.astype(BF16), v)]
        return tuple(out)

    state = lax.fori_loop(0, last, body, tuple(state))
    for h in range(heads):
        o_ref[:, head_cols(h)] = (state[3 * h + 2] / state[3 * h + 1]).astype(o_ref.dtype)


def _moba(proj, casts, *, batch, seq, col0, heads_per_step=4):
    blk = MOBA_BLOCK
    nq = seq // blk
    assert nq % 2 == 0
    nb_pad = -(-nq // 16) * 16
    assert nb_pad <= HEAD_DIM
    width = heads_per_step * HEAD_DIM
    c0 = col0 // width
    per = ATT_HEADS // heads_per_step
    kern = functools.partial(_moba_kernel, seq=seq, heads=heads_per_step)
    return _call_with_casts(
        kern, casts,
        grid=(batch, per, nq),
        in_specs=[
            pl.BlockSpec((blk, width), lambda b, h, i: (b * nq + i, c0 + h)),
            pl.BlockSpec((seq, width), lambda b, h, i: (b, c0 + per + h)),
            pl.BlockSpec((seq, width), lambda b, h, i: (b, c0 + 2 * per + h)),
        ],
        out_specs=pl.BlockSpec((blk, width), lambda b, h, i: (b * nq + i, h)),
        out_shape=jax.ShapeDtypeStruct((batch * seq, ATT_HEADS * HEAD_DIM), BF16),
        scratch_shapes=[
            pltpu.VMEM((heads_per_step, seq, 2 * HEAD_DIM), BF16),
            pltpu.VMEM((heads_per_step, 3, nb_pad, HEAD_DIM), BF16),
        ],
        compiler_params=_params(("arbitrary", "arbitrary", "arbitrary")),
        name="moba",
    )(proj, proj, proj)


def _rope_tables(seq):
    inv = 1.0 / (ROPE_THETA ** (jnp.arange(0, HEAD_DIM, 2, dtype=F32) / HEAD_DIM))
    ang = jnp.arange(seq, dtype=F32)[:, None] * inv[None, :]
    cos, sin = jnp.cos(ang), jnp.sin(ang)
    return jnp.concatenate([cos, cos], axis=-1), jnp.concatenate([-sin, sin], axis=-1)


def _colscale(n, q_ranges):
    cs = jnp.ones((1, n), F32)
    for lo, hi in q_ranges:
        cs = cs.at[:, lo:hi].set(Q_SCALE)
    return cs


def kernel(x, norm_gains, w_in_ab, pool_w, pool_scale, w_out_ab, w_in_cd, w_out_cd, ffn_gate, ffn_up, ffn_down):
    batch, seq, d = x.shape
    att = ATT_HEADS * HEAD_DIM
    pool_width = len(POOL_WINDOWS) * POOL_GROUP
    n_groups = len(DIL_PATTERNS)
    cos, sin = _rope_tables(seq)
    gains = norm_gains.reshape(norm_gains.shape[0], 4, 1, d)
    xs = x.reshape(batch * seq, d)

    qkv0 = pool_width
    proj = _inproj(xs, gains[0, 0], w_in_ab[0].astype(BF16),
                   _colscale(w_in_ab.shape[2], [(qkv0, qkv0 + n_groups * att)]), cos, sin, seq=seq,
                   rope_cols=(qkv0, qkv0 + 2 * n_groups * att))
    a_out, (w_out0, w_in1) = _pool(proj, pool_w[0].astype(BF16), pool_scale[0].reshape(1, pool_width),
                                   [(w_out_ab, 0), (w_in_cd, 0)], seq=seq)
    ffn_weights = (ffn_gate, ffn_up, ffn_down)
    b_out, ffn_w = _dilated(proj, [(w, 0) for w in ffn_weights], batch=batch, seq=seq, col0=qkv0)
    xs = _outproj(a_out, b_out, w_out0, xs, gains[0, 1])
    xs = _ffn(xs, gains[0, 2], gains[0, 3], *ffn_w)

    moba0 = 3 * att
    proj = _inproj(xs, gains[1, 0], w_in1,
                   _colscale(w_in_cd.shape[2], [(0, att), (moba0, moba0 + att)]), cos, sin, seq=seq,
                   rope_cols=(moba0, moba0 + 2 * att))
    c_out, ffn_w = _stick_breaking(proj, [(w, 1) for w in ffn_weights], batch=batch, seq=seq, col0=0)
    d_out, (w_out1,) = _moba(proj, [(w_out_cd, 0)], batch=batch, seq=seq, col0=moba0)
    xs = _outproj(c_out, d_out, w_out1, xs, gains[1, 1])
    xs = _ffn(xs, gains[1, 2], gains[1, 3], *ffn_w)
    return xs.reshape(batch, seq, d)
```

```python
import functools

import jax
import jax.numpy as jnp
from jax import lax
from jax.experimental import pallas as pl
from jax.experimental.pallas import tpu as pltpu

F32 = jnp.float32
BF16 = jnp.bfloat16

HEAD_DIM = 128
EPS = 1e-6
ROPE_THETA = 10000.0
POOL_WINDOWS = (2, 4, 8, 16)
POOL_GROUP = 256
POOL_HALO = 16
DIL_PATTERNS = ((128, 1), (512, 4), (2048, 16))
DIL_QBLOCK = 128
DIL_UNROLL = 8
ATT_HEADS = 8
MOBA_BLOCK = 256
MOBA_TOPK = 3
SB_BLOCK = 256
SB_TILE = 2 * SB_BLOCK
SB_LINEAR_ABOVE = 64.0
LOG2E = 1.4426950408889634
Q_SCALE = LOG2E * HEAD_DIM ** -0.5
MASK_VALUE = -1e30
MOBA_UNSELECTED = -30000.0

V7X_VMEM_BYTES = 64 * 1024 * 1024
VMEM_LIMIT = V7X_VMEM_BYTES - 8 * 1024 * 1024


def _params(semantics):
    return pltpu.CompilerParams(dimension_semantics=semantics, vmem_limit_bytes=VMEM_LIMIT)


def _dot(a, b):
    return jnp.dot(a, b, preferred_element_type=F32)


def _dot_nt(a, b):
    return lax.dot_general(a, b, (((1,), (1,)), ((), ())), preferred_element_type=F32)


def _rms_norm(x, gain):
    return x * lax.rsqrt(jnp.mean(x * x, axis=-1, keepdims=True) + EPS) * gain


def _row_blocks(rows, at_most):
    return max(n for n in range(1, at_most + 1) if rows % (16 * n) == 0)


def _call_with_casts(kernel, casts, *, grid, in_specs, out_specs, out_shape, together=False, **kwargs):
    steps = 1
    for g in grid:
        steps *= g
    share = 1 if together else max(len(casts), 1)

    def flat(idx):
        s = idx[0]
        for g, i in zip(grid[1:], idx[1:]):
            s = s * g + i
        return s

    jobs = []
    cast_in, cast_out, cast_shapes = [], [], []
    for w, layer in casts:
        _, rows, cols = w.shape
        count = _row_blocks(rows, max(steps // share, 1))
        first = 0 if together else sum(c for _, c in jobs)
        jobs.append((first, count))

        def block(*idx, first=first, count=count):
            return jnp.clip(flat(idx) - first, 0, count - 1)

        cast_in.append(pl.BlockSpec((None, rows // count, cols), lambda *idx, block=block, layer=layer:
                                    (layer, block(*idx), 0)))
        cast_out.append(pl.BlockSpec((rows // count, cols), lambda *idx, block=block: (block(*idx), 0)))
        cast_shapes.append(jax.ShapeDtypeStruct((rows, cols), BF16))

    n_in, n_cast = len(in_specs), len(casts)

    def with_casts(*refs):
        main_in, src = refs[:n_in], refs[n_in:n_in + n_cast]
        main_out, dst = refs[n_in + n_cast], refs[n_in + n_cast + 1:n_in + 2 * n_cast + 1]
        kernel(*main_in, main_out, *refs[n_in + 2 * n_cast + 1:])
        step = flat([pl.program_id(a) for a in range(len(grid))])
        for s, d, (first, count) in zip(src, dst, jobs):
            @pl.when(jnp.logical_and(step >= first, step < first + count))
            def _(s=s, d=d):
                d[...] = s[...].astype(BF16)

    call = pl.pallas_call(
        with_casts, grid=grid, in_specs=list(in_specs) + cast_in, out_specs=[out_specs] + cast_out,
        out_shape=[out_shape] + cast_shapes, **kwargs)

    def run(*args):
        res = call(*args, *[w for w, _ in casts])
        return res[0], list(res[1:])

    return run


def _inproj_kernel(x_ref, g_ref, w_ref, cs_ref, cos_ref, sin_ref, o_ref, hn_ref, *, rope_lo, rope_hi):
    j = pl.program_id(1)

    @pl.when(j == 0)
    def _():
        hn_ref[...] = _rms_norm(x_ref[...], g_ref[...]).astype(BF16)

    acc = _dot(hn_ref[...], w_ref[...]) * cs_ref[...]
    is_rope = jnp.logical_and(j >= rope_lo, j < rope_hi)
    cos = jnp.where(is_rope, cos_ref[...], 1.0)
    sin = jnp.where(is_rope, sin_ref[...], 0.0)
    for c in range(acc.shape[1] // HEAD_DIM):
        blk = acc[:, c * HEAD_DIM:(c + 1) * HEAD_DIM]
        rot = pltpu.roll(blk, HEAD_DIM // 2, 1)
        o_ref[:, c * HEAD_DIM:(c + 1) * HEAD_DIM] = (blk * cos + rot * sin).astype(o_ref.dtype)


def _inproj(x, gain, w, colscale, cos, sin, *, seq, rope_cols, tm=1024, tn=1024):
    t, d = x.shape
    n = w.shape[1]
    assert t % tm == 0 and n % tn == 0 and seq % tm == 0
    assert rope_cols[0] % tn == 0 and rope_cols[1] % tn == 0
    pos_blocks = seq // tm
    kern = functools.partial(_inproj_kernel, rope_lo=rope_cols[0] // tn, rope_hi=rope_cols[1] // tn)
    return pl.pallas_call(
        kern,
        grid=(t // tm, n // tn),
        in_specs=[
            pl.BlockSpec((tm, d), lambda i, j: (i, 0)),
            pl.BlockSpec((1, d), lambda i, j: (0, 0)),
            pl.BlockSpec((d, tn), lambda i, j: (0, j)),
            pl.BlockSpec((1, tn), lambda i, j: (0, j)),
            pl.BlockSpec((tm, HEAD_DIM), lambda i, j: (i % pos_blocks, 0)),
            pl.BlockSpec((tm, HEAD_DIM), lambda i, j: (i % pos_blocks, 0)),
        ],
        out_specs=pl.BlockSpec((tm, tn), lambda i, j: (i, j)),
        out_shape=jax.ShapeDtypeStruct((t, n), BF16),
        scratch_shapes=[pltpu.VMEM((tm, d), BF16)],
        compiler_params=_params(("parallel", "arbitrary")),
        name="inproj",
    )(x, gain, w, colscale, cos, sin)


def _pool_kernel(u_ref, halo_ref, w_ref, sc_ref, o_ref, ext_ref, lvl_a_ref, lvl_b_ref, *, tm, seq):
    halo, lead = POOL_HALO, 2 * POOL_HALO
    span = tm + halo
    pos0 = (pl.program_id(0) * tm) % seq
    ext_ref[pl.ds(lead, tm), :] = u_ref[...].astype(F32)
    ext_ref[pl.ds(halo, halo), :] = jnp.where(pos0 == 0, 0.0, halo_ref[...].astype(F32))
    ext_ref[pl.ds(0, halo), :] = jnp.zeros((halo, ext_ref.shape[1]), F32)
    lvl_a_ref[pl.ds(0, halo), :] = jnp.zeros((halo, POOL_GROUP), F32)
    lvl_b_ref[pl.ds(0, halo), :] = jnp.zeros((halo, POOL_GROUP), F32)
    pos = pos0 + lax.broadcasted_iota(jnp.int32, (tm, 1), 0)

    def doubled(src_ref, cols, start, rows, shift):
        return src_ref[pl.ds(start, rows), cols] + src_ref[pl.ds(start - shift, rows), cols]

    for g, w in enumerate(POOL_WINDOWS):
        cols = pl.ds(g * POOL_GROUP, POOL_GROUP)
        cur = ext_ref[pl.ds(lead, tm), cols]
        src, src_cols, shift = ext_ref, cols, 1
        levels = [lvl_a_ref, lvl_b_ref]
        while 2 * shift < w:
            dst = levels[0]
            levels.reverse()
            dst[pl.ds(halo, span), :] = doubled(src, src_cols, halo, span, shift)
            src, src_cols, shift = dst, slice(None), 2 * shift
        tot = doubled(src, src_cols, lead, tm, shift)
        cnt = jnp.minimum(pos + 1, w).astype(F32)
        pooled = tot / cnt - cur
        y = _dot(pooled.astype(BF16), w_ref[g])
        o_ref[:, g * POOL_GROUP:(g + 1) * POOL_GROUP] = (
            y * sc_ref[:, g * POOL_GROUP:(g + 1) * POOL_GROUP]).astype(o_ref.dtype)


def _pool(proj, w, scale, casts, *, seq, tm=512):
    t = proj.shape[0]
    width = len(POOL_WINDOWS) * POOL_GROUP
    assert seq % tm == 0 and tm % POOL_HALO == 0
    halo_blocks = tm // POOL_HALO
    kern = functools.partial(_pool_kernel, tm=tm, seq=seq)
    return _call_with_casts(
        kern, casts, together=True,
        grid=(t // tm,),
        in_specs=[
            pl.BlockSpec((tm, width), lambda i: (i, 0)),
            pl.BlockSpec((POOL_HALO, width), lambda i: (jnp.maximum(i * halo_blocks - 1, 0), 0)),
            pl.BlockSpec(w.shape, lambda i: (0, 0, 0)),
            pl.BlockSpec((1, width), lambda i: (0, 0)),
        ],
        out_specs=pl.BlockSpec((tm, width), lambda i: (i, 0)),
        out_shape=jax.ShapeDtypeStruct((t, width), BF16),
        scratch_shapes=[pltpu.VMEM((tm + 2 * POOL_HALO, width), F32),
                        pltpu.VMEM((tm + 2 * POOL_HALO, POOL_GROUP), F32),
                        pltpu.VMEM((tm + 2 * POOL_HALO, POOL_GROUP), F32)],
        compiler_params=_params(("arbitrary",)),
        name="pool",
    )(proj, proj, w, scale)


def _dil_kernel(q_ref, k_ref, v_ref, o_ref, qd_ref, kd_ref, vd_ref, qw_ref, kw_ref, vw_ref,
                m_ref, l_ref, acc_ref, *, seq):
    g = pl.program_id(2)
    wide_refs = (qw_ref, kw_ref, vw_ref)
    qb = DIL_QBLOCK
    nblocks = seq // qb

    @pl.when(g == 0)
    def _():
        m_ref[...] = jnp.full(m_ref.shape, MASK_VALUE, F32)
        l_ref[...] = jnp.zeros(l_ref.shape, F32)
        acc_ref[...] = jnp.zeros(acc_ref.shape, F32)
        kd_ref[pl.ds(0, qb), :] = jnp.zeros((qb, HEAD_DIM), BF16)
        vd_ref[pl.ds(0, qb), :] = jnp.zeros((qb, HEAD_DIM), BF16)

    row = lax.broadcasted_iota(jnp.int32, (qb, 2 * qb), 0)
    col = lax.broadcasted_iota(jnp.int32, (qb, 2 * qb), 1)
    band = jnp.logical_and(col >= row, col <= row + qb)
    own_block = col >= qb

    def run_group(dil):
        nblk = seq // dil // qb
        assert nblk & (nblk - 1) == 0
        shift = nblk.bit_length() - 1

        def positions(idx):
            r = lax.shift_right_logical(idx, shift)
            n = jnp.bitwise_and(idx, nblk - 1)
            start = r + dil * qb * n
            return (pl.ds(start, qb, stride=dil) if dil > 1 else pl.ds(pl.multiple_of(start, qb), qb)), n

        if dil > 1:
            def widen(idx, carry):
                rows = pl.ds(pl.multiple_of(idx * qb, qb), qb)
                for src_ref, wide_ref in zip((q_ref, k_ref, v_ref), wide_refs):
                    wide_ref[rows, :] = src_ref[rows, :].astype(F32)
                return carry

            lax.fori_loop(0, nblocks, widen, 0, unroll=4)
            sources = wide_refs
        else:
            sources = (q_ref, k_ref, v_ref)

        def gather(idx, carry):
            src, _ = positions(idx)
            qd_ref[pl.ds(pl.multiple_of(idx * qb, qb), qb), :] = sources[0][src, :].astype(BF16)
            dst = pl.ds(pl.multiple_of((idx + 1) * qb, qb), qb)
            kd_ref[dst, :] = sources[1][src, :].astype(BF16)
            vd_ref[dst, :] = sources[2][src, :].astype(BF16)
            return carry

        lax.fori_loop(0, nblocks, gather, 0, unroll=4)

        def window(idx):
            return pl.ds(pl.multiple_of(idx * qb, qb), 2 * qb)

        def block_scores(idx):
            q = qd_ref[pl.ds(pl.multiple_of(idx * qb, qb), qb), :]
            return _dot_nt(q, kd_ref[window(idx), :])

        def block_update(idx, s):
            cur, n = positions(idx)
            s = jnp.where(jnp.logical_and(band, jnp.logical_or(own_block, n > 0)), s, MASK_VALUE)
            m_old = m_ref[cur, :]
            m_new = jnp.maximum(m_old, jnp.max(s, axis=-1, keepdims=True))
            p = jnp.exp2(s - jnp.concatenate([m_new, m_new], axis=1))
            alpha = jnp.exp2(m_old - m_new)
            l_ref[cur, :] = alpha * l_ref[cur, :] + jnp.sum(p, axis=-1, keepdims=True)
            acc_ref[cur, :] = alpha * acc_ref[cur, :] + _dot(p.astype(BF16), vd_ref[window(idx), :])
            m_ref[cur, :] = m_new

        def body(it, carry):
            first = it * DIL_UNROLL
            scores = [block_scores(first + u) for u in range(DIL_UNROLL)]
            for u in range(DIL_UNROLL):
                block_update(first + u, scores[u])
            return carry

        lax.fori_loop(0, nblocks // DIL_UNROLL, body, 0)

    for gi, (_, dil) in enumerate(DIL_PATTERNS):
        pl.when(g == gi)(functools.partial(run_group, dil))

    @pl.when(g == len(DIL_PATTERNS) - 1)
    def _():
        o_ref[...] = (acc_ref[...] / l_ref[...]).astype(o_ref.dtype)


def _dilated(proj, casts, *, batch, seq, col0):
    n_groups = len(DIL_PATTERNS)
    per = n_groups * ATT_HEADS
    c0 = col0 // HEAD_DIM
    kern = functools.partial(_dil_kernel, seq=seq)

    def spec(which):
        return pl.BlockSpec((seq, HEAD_DIM), lambda b, h, g: (b, c0 + which * per + g * ATT_HEADS + h))

    return _call_with_casts(
        kern, casts,
        grid=(batch, ATT_HEADS, n_groups),
        in_specs=[spec(0), spec(1), spec(2)],
        out_specs=pl.BlockSpec((seq, HEAD_DIM), lambda b, h, g: (b, h)),
        out_shape=jax.ShapeDtypeStruct((batch * seq, ATT_HEADS * HEAD_DIM), BF16),
        scratch_shapes=[
            pltpu.VMEM((seq, HEAD_DIM), BF16),
            pltpu.VMEM((seq + DIL_QBLOCK, HEAD_DIM), BF16),
            pltpu.VMEM((seq + DIL_QBLOCK, HEAD_DIM), BF16),
        ] + [pltpu.VMEM((seq, HEAD_DIM), F32)] * 6,
        compiler_params=_params(("arbitrary", "arbitrary", "arbitrary")),
        name="dilated",
    )(proj, proj, proj)


def _outproj_kernel(a_ref, b_ref, wa_ref, wb_ref, x_ref, g_ref, o_ref):
    y = _dot(a_ref[...], wa_ref[...]) + _dot(b_ref[...], wb_ref[...])
    o_ref[...] = x_ref[...] + _rms_norm(y, g_ref[...])


def _outproj(a, b, w, x, gain, *, tm=512):
    t, d = x.shape
    half = a.shape[1]
    assert w.shape == (2 * half, d) and t % tm == 0
    return pl.pallas_call(
        _outproj_kernel,
        grid=(t // tm,),
        in_specs=[
            pl.BlockSpec((tm, half), lambda i: (i, 0)),
            pl.BlockSpec((tm, half), lambda i: (i, 0)),
            pl.BlockSpec((half, d), lambda i: (0, 0)),
            pl.BlockSpec((half, d), lambda i: (1, 0)),
            pl.BlockSpec((tm, d), lambda i: (i, 0)),
            pl.BlockSpec((1, d), lambda i: (0, 0)),
        ],
        out_specs=pl.BlockSpec((tm, d), lambda i: (i, 0)),
        out_shape=jax.ShapeDtypeStruct((t, d), F32),
        compiler_params=_params(("parallel",)),
        name="outproj",
    )(a, b, w, w, x, gain)


def _ffn_kernel(x_ref, g_in_ref, g_out_ref, wg_ref, wu_ref, wd_ref, o_ref, hn_ref, acc_ref):
    j = pl.program_id(1)

    @pl.when(j == 0)
    def _():
        hn_ref[...] = _rms_norm(x_ref[...], g_in_ref[...]).astype(BF16)
        acc_ref[...] = jnp.zeros(acc_ref.shape, F32)

    h = hn_ref[...]
    gate = _dot(h, wg_ref[...])
    up = _dot(h, wu_ref[...])
    act = (gate * jax.nn.sigmoid(gate)) * up
    acc_ref[...] += _dot(act.astype(BF16), wd_ref[...])

    @pl.when(j == pl.num_programs(1) - 1)
    def _():
        o_ref[...] = x_ref[...] + _rms_norm(acc_ref[...], g_out_ref[...])


def _ffn(x, g_in, g_out, wg, wu, wd, *, tm=512, tf=512):
    t, d = x.shape
    f = wg.shape[1]
    assert t % tm == 0 and f % tf == 0
    return pl.pallas_call(
        _ffn_kernel,
        grid=(t // tm, f // tf),
        in_specs=[
            pl.BlockSpec((tm, d), lambda i, j: (i, 0)),
            pl.BlockSpec((1, d), lambda i, j: (0, 0)),
            pl.BlockSpec((1, d), lambda i, j: (0, 0)),
            pl.BlockSpec((d, tf), lambda i, j: (0, j)),
            pl.BlockSpec((d, tf), lambda i, j: (0, j)),
            pl.BlockSpec((tf, d), lambda i, j: (j, 0)),
        ],
        out_specs=pl.BlockSpec((tm, d), lambda i, j: (i, 0)),
        out_shape=jax.ShapeDtypeStruct((t, d), F32),
        scratch_shapes=[pltpu.VMEM((tm, d), BF16), pltpu.VMEM((tm, d), F32)],
        compiler_params=_params(("parallel", "arbitrary")),
        name="ffn",
    )(x, g_in, g_out, wg, wu, wd)


def _sb_suffix(z, tri, mask):
    tb = SB_BLOCK
    soft = jnp.where(z > SB_LINEAR_ABOVE, z, jnp.log(1.0 + jnp.exp2(z)) * LOG2E)
    if mask is not None:
        soft = jnp.where(mask, soft, 0.0)
    p16 = soft.astype(BF16)
    left, right = slice(0, tb), slice(tb, 2 * tb)
    suffix_l = _dot(p16[:, left], tri)
    suffix_r = _dot(p16[:, right], tri)
    total_l = jnp.sum(soft[:, left], axis=-1, keepdims=True)
    total_r = jnp.sum(soft[:, right], axis=-1, keepdims=True)
    return suffix_l, suffix_r, total_l, total_r


def _sb_weights(z, sums, carry, mask):
    tb = SB_BLOCK
    suffix_l, suffix_r, total_l, total_r = sums
    expo = jnp.concatenate([z[:, :tb] - suffix_l - (carry + total_r), z[:, tb:] - suffix_r - carry], axis=1)
    a = jnp.exp2(expo)
    if mask is not None:
        a = jnp.where(mask, a, 0.0)
    return a.astype(BF16), carry + total_r + total_l


def _sb_kernel(q_ref, k_ref, v_ref, tri_ref, o_ref, *, heads):
    qi = pl.program_id(2)
    tq = SB_TILE
    tri = tri_ref[...]
    row = lax.broadcasted_iota(jnp.int32, (tq, tq), 0)
    col = lax.broadcasted_iota(jnp.int32, (tq, tq), 1)
    strictly_past = col < row

    def head_cols(h):
        return slice(h * HEAD_DIM, (h + 1) * HEAD_DIM)

    qs = [q_ref[:, head_cols(h)] for h in range(heads)]

    def tiles(rows, accs, carries, mask):
        zs = [_dot_nt(qs[h], k_ref[rows, head_cols(h)]) for h in range(heads)]
        sums = [_sb_suffix(zs[h], tri, mask) for h in range(heads)]
        out = []
        for h in range(heads):
            a, carry = _sb_weights(zs[h], sums[h], carries[h], mask)
            out += [accs[h] + _dot(a, v_ref[rows, head_cols(h)]), carry]
        return out

    diag = pl.ds(pl.multiple_of(qi * tq, tq), tq)
    state = tiles(diag, [jnp.zeros((tq, HEAD_DIM), F32)] * heads, [jnp.zeros((tq, 1), F32)] * heads, strictly_past)

    def body(it, st):
        rows = pl.ds(pl.multiple_of((qi - 1 - it) * tq, tq), tq)
        return tuple(tiles(rows, st[0::2], st[1::2], None))

    state = lax.fori_loop(0, qi, body, tuple(state))
    for h in range(heads):
        o_ref[:, head_cols(h)] = state[2 * h].astype(o_ref.dtype)


def _stick_breaking(proj, casts, *, batch, seq, col0, heads_per_step=2):
    tb, tq = SB_BLOCK, SB_TILE
    assert tq == 2 * tb and seq % tq == 0
    width = heads_per_step * HEAD_DIM
    c0 = col0 // width
    per = ATT_HEADS // heads_per_step
    nq = seq // tq
    tri = (lax.broadcasted_iota(jnp.int32, (tb, tb), 0) >= lax.broadcasted_iota(jnp.int32, (tb, tb), 1)).astype(BF16)
    kern = functools.partial(_sb_kernel, heads=heads_per_step)
    return _call_with_casts(
        kern, casts,
        grid=(batch, per, nq),
        in_specs=[
            pl.BlockSpec((tq, width), lambda b, h, i: (b * nq + i, c0 + h)),
            pl.BlockSpec((seq, width), lambda b, h, i: (b, c0 + per + h)),
            pl.BlockSpec((seq, width), lambda b, h, i: (b, c0 + 2 * per + h)),
            pl.BlockSpec((tb, tb), lambda b, h, i: (0, 0)),
        ],
        out_specs=pl.BlockSpec((tq, width), lambda b, h, i: (b * nq + i, h)),
        out_shape=jax.ShapeDtypeStruct((batch * seq, ATT_HEADS * HEAD_DIM), BF16),
        compiler_params=_params(("arbitrary", "arbitrary", "arbitrary")),
        name="stick_breaking",
    )(proj, proj, proj, tri)


def _moba_setup(k, kaug_ref, kmean_ref, *, seq):
    blk = MOBA_BLOCK
    nb = kmean_ref.shape[1]
    lane = lax.broadcasted_iota(jnp.int32, (blk, HEAD_DIM), 1)
    for n in range(seq // blk):
        kaug_ref[n * blk:(n + 1) * blk, :HEAD_DIM] = k[n * blk:(n + 1) * blk, :]
        kaug_ref[n * blk:(n + 1) * blk, HEAD_DIM:] = jnp.where(lane == n, 1.0, 0.0).astype(BF16)
    key = lax.broadcasted_iota(jnp.int32, (nb, seq), 1)
    first = lax.broadcasted_iota(jnp.int32, (nb, seq), 0) * blk
    in_block = jnp.logical_and(key >= first, key < first + blk)
    mean = _dot(jnp.where(in_block, 1.0 / blk, 0.0).astype(BF16), k)
    hi = mean.astype(BF16)
    mid = (mean - hi.astype(F32)).astype(BF16)
    lo = (mean - hi.astype(F32) - mid.astype(F32)).astype(BF16)
    kmean_ref[0] = hi
    kmean_ref[1] = mid
    kmean_ref[2] = lo


def _moba_select(q, qi, kmean_ref):
    blk = MOBA_BLOCK
    nb = kmean_ref.shape[1]
    gate = _dot_nt(kmean_ref[0], q) + _dot_nt(kmean_ref[1], q) + _dot_nt(kmean_ref[2], q)
    blk_id = lax.broadcasted_iota(jnp.int32, (nb, blk), 0)
    past = blk_id < qi
    gate = jnp.where(past, gate, -jnp.inf)
    beaten = jnp.zeros((nb, blk), F32)
    for m in range(nb):
        gm = gate[m:m + 1, :]
        ahead = jnp.logical_or(gm > gate, jnp.logical_and(gm == gate, blk_id > m))
        beaten = beaten + jnp.where(ahead, 1.0, 0.0)
    keep = jnp.logical_or(jnp.logical_and(past, beaten < MOBA_TOPK), blk_id == qi)
    bias = jnp.where(keep, 0.0, MOBA_UNSELECTED)
    bias = jnp.concatenate([bias, jnp.zeros((HEAD_DIM - nb, blk), F32)], axis=0)
    return jnp.concatenate([q, bias.T.astype(BF16)], axis=1)


def _moba_kernel(q_ref, k_ref, v_ref, o_ref, kaug_ref, kmean_ref, *, seq, heads):
    qi = pl.program_id(2)
    blk = MOBA_BLOCK
    pair = 2 * blk

    def head_cols(h):
        return slice(h * HEAD_DIM, (h + 1) * HEAD_DIM)

    @pl.when(qi == 0)
    def _():
        for h in range(heads):
            _moba_setup(k_ref[:, head_cols(h)], kaug_ref.at[h], kmean_ref.at[h], seq=seq)

    qa = [_moba_select(q_ref[:, head_cols(h)], qi, kmean_ref.at[h]) for h in range(heads)]

    def scores(h, j):
        rows = pl.ds(pl.multiple_of(j * pair, pair), pair)
        return _dot_nt(qa[h], kaug_ref[h, rows, :]), v_ref[rows, head_cols(h)]

    last = lax.shift_right_logical(qi, 1)
    row = lax.broadcasted_iota(jnp.int32, (blk, pair), 0)
    col = lax.broadcasted_iota(jnp.int32, (blk, pair), 1)
    causal = col <= row + (qi - 2 * last) * blk
    state = []
    for h in range(heads):
        s, v = scores(h, last)
        s = jnp.where(causal, s, MASK_VALUE)
        m = jnp.max(s, axis=-1, keepdims=True)
        p = jnp.exp2(s - m)
        state += [m, jnp.sum(p, axis=-1, keepdims=True), _dot(p.astype(BF16), v)]

    def body(j, st):
        out = []
        nxt = scores(0, j)
        for h in range(heads):
            s, v = nxt
            if h + 1 < heads:
                nxt = scores(h + 1, j)
            m_old, l_old, acc = st[3 * h:3 * h + 3]
            m_new = jnp.maximum(m_old, jnp.max(s, axis=-1, keepdims=True))
            alpha = jnp.exp2(m_old - m_new)
            p = jnp.exp2(s - m_new)
            out += [m_new, alpha * l_old + jnp.sum(p, axis=-1, keepdims=True),
                    alpha * acc + _dot(p.astype(BF16), v)]
        return tuple(out)

    state = lax.fori_loop(0, last, body, tuple(state))
    for h in range(heads):
        o_ref[:, head_cols(h)] = (state[3 * h + 2] / state[3 * h + 1]).astype(o_ref.dtype)


def _moba(proj, casts, *, batch, seq, col0, heads_per_step=4):
    blk = MOBA_BLOCK
    nq = seq // blk
    assert nq % 2 == 0
    nb_pad = -(-nq // 16) * 16
    assert nb_pad <= HEAD_DIM
    width = heads_per_step * HEAD_DIM
    c0 = col0 // width
    per = ATT_HEADS // heads_per_step
    kern = functools.partial(_moba_kernel, seq=seq, heads=heads_per_step)
    return _call_with_casts(
        kern, casts,
        grid=(batch, per, nq),
        in_specs=[
            pl.BlockSpec((blk, width), lambda b, h, i: (b * nq + i, c0 + h)),
            pl.BlockSpec((seq, width), lambda b, h, i: (b, c0 + per + h)),
            pl.BlockSpec((seq, width), lambda b, h, i: (b, c0 + 2 * per + h)),
        ],
        out_specs=pl.BlockSpec((blk, width), lambda b, h, i: (b * nq + i, h)),
        out_shape=jax.ShapeDtypeStruct((batch * seq, ATT_HEADS * HEAD_DIM), BF16),
        scratch_shapes=[
            pltpu.VMEM((heads_per_step, seq, 2 * HEAD_DIM), BF16),
            pltpu.VMEM((heads_per_step, 3, nb_pad, HEAD_DIM), BF16),
        ],
        compiler_params=_params(("arbitrary", "arbitrary", "arbitrary")),
        name="moba",
    )(proj, proj, proj)


def _rope_tables(seq):
    inv = 1.0 / (ROPE_THETA ** (jnp.arange(0, HEAD_DIM, 2, dtype=F32) / HEAD_DIM))
    ang = jnp.arange(seq, dtype=F32)[:, None] * inv[None, :]
    cos, sin = jnp.cos(ang), jnp.sin(ang)
    return jnp.concatenate([cos, cos], axis=-1), jnp.concatenate([-sin, sin], axis=-1)


def _colscale(n, q_ranges):
    cs = jnp.ones((1, n), F32)
    for lo, hi in q_ranges:
        cs = cs.at[:, lo:hi].set(Q_SCALE)
    return cs


def kernel(x, norm_gains, w_in_ab, pool_w, pool_scale, w_out_ab, w_in_cd, w_out_cd, ffn_gate, ffn_up, ffn_down):
    batch, seq, d = x.shape
    att = ATT_HEADS * HEAD_DIM
    pool_width = len(POOL_WINDOWS) * POOL_GROUP
    n_groups = len(DIL_PATTERNS)
    cos, sin = _rope_tables(seq)
    gains = norm_gains.reshape(norm_gains.shape[0], 4, 1, d)
    xs = x.reshape(batch * seq, d)

    qkv0 = pool_width
    proj = _inproj(xs, gains[0, 0], w_in_ab[0].astype(BF16),
                   _colscale(w_in_ab.shape[2], [(qkv0, qkv0 + n_groups * att)]), cos, sin, seq=seq,
                   rope_cols=(qkv0, qkv0 + 2 * n_groups * att))
    a_out, (w_out0, w_in1) = _pool(proj, pool_w[0].astype(BF16), pool_scale[0].reshape(1, pool_width),
                                   [(w_out_ab, 0), (w_in_cd, 0)], seq=seq)
    ffn_weights = (ffn_gate, ffn_up, ffn_down)
    b_out, ffn_w = _dilated(proj, [(w, 0) for w in ffn_weights], batch=batch, seq=seq, col0=qkv0)
    xs = _outproj(a_out, b_out, w_out0, xs, gains[0, 1])
    xs = _ffn(xs, gains[0, 2], gains[0, 3], *ffn_w)

    moba0 = 3 * att
    proj = _inproj(xs, gains[1, 0], w_in1,
                   _colscale(w_in_cd.shape[2], [(0, att), (moba0, moba0 + att)]), cos, sin, seq=seq,
                   rope_cols=(moba0, moba0 + 2 * att))
    c_out, ffn_w = _stick_breaking(proj, [(w, 1) for w in ffn_weights], batch=batch, seq=seq, col0=0)
    d_out, (w_out1,) = _moba(proj, [(w_out_cd, 0)], batch=batch, seq=seq, col0=moba0)
    xs = _outproj(c_out, d_out, w_out1, xs, gains[1, 1])
    xs = _ffn(xs, gains[1, 2], gains[1, 3], *ffn_w)
    return xs.reshape(batch, seq, d)
```
